```python
import functools
import jax, jax.numpy as jnp
from jax import lax
import numpy as np

D_MODEL = 2048
BATCH = 16
SEQ = 2048
DEPTH = 4
DEC_BATCH = 16
DEC_SEQ = 64
PAST_LEN = 2048

CHUNK = 64
N_MIXERS = 2
N_POOL_LAYERS = (DEPTH + 1) // 2
N_SGU_LAYERS = DEPTH // 2
D_POOL = D_MODEL
POOL_WINDOWS = (2, 4, 8, 16)
N_POOL_GROUPS = len(POOL_WINDOWS)
POOL_GROUP_DIM = D_POOL // N_POOL_GROUPS
POOL_STATE = POOL_WINDOWS[-1] - 1
SGU_BLOCK = 128
D_SGU = D_MODEL
SGU_HEADS = 8
SGU_HEAD_DIM = D_SGU // SGU_HEADS
D_FF = 5632
N_EXPERTS = 8
TOP_K = 2
D_EXPERT = 7168
N_MOD = 6
EPS = 1e-6

kernel_name = 'hybrid_pool_sgu_streaming_encoder_step'


def rmsnorm(x, g):
    xf = x.astype(jnp.float32)
    y = xf * lax.rsqrt(jnp.mean(xf * xf, axis=-1, keepdims=True) + EPS)
    return (y * g.astype(jnp.float32)).astype(x.dtype)


def layernorm(x, g, b):
    xf = x.astype(jnp.float32)
    mu = jnp.mean(xf, axis=-1, keepdims=True)
    xc = xf - mu
    y = xc * lax.rsqrt(jnp.mean(xc * xc, axis=-1, keepdims=True) + EPS)
    return (y * g.astype(jnp.float32) + b.astype(jnp.float32)).astype(x.dtype)


def adaln_mod(c, w, b):
    mod = jax.nn.silu(c) @ w + b
    return [m[:, None, :] for m in jnp.split(mod, N_MOD, axis=-1)]


def pre_norm(x, g, shift, scale):
    return rmsnorm(x, g) * (1 + scale) + shift


def post_add(x, y, g, gate):
    return x + gate * rmsnorm(y, g)


def pool_mix(u_ext, n_hist, w_grp, scale):
    L_ext = u_ext.shape[1]
    uf = u_ext.astype(jnp.float32)
    csum = jnp.cumsum(uf, axis=1)
    pos = jnp.arange(L_ext)
    outs = []
    for gi, w in enumerate(POOL_WINDOWS):
        sl = slice(gi * POOL_GROUP_DIM, (gi + 1) * POOL_GROUP_DIM)
        cg = csum[..., sl]
        lag = jnp.pad(cg, ((0, 0), (w, 0), (0, 0)))[:, :L_ext]
        cnt = jnp.minimum(pos + 1, w).astype(jnp.float32)[None, :, None]
        outs.append(((cg - lag) / cnt - uf[..., sl])[:, n_hist:])
    pooled = jnp.stack(outs, axis=2).astype(u_ext.dtype)
    mixed = jnp.einsum('blgc,gcd->blgd', pooled, w_grp)
    B, L = mixed.shape[:2]
    return mixed.reshape(B, L, D_POOL) * scale


def pool_mixer(h, hist, w_in, w_grp, scale, w_out):
    u = h @ w_in
    if hist is None:
        u_ext, n_hist = u, 0
    else:
        u_ext = jnp.concatenate([hist.astype(u.dtype), u], axis=1)
        n_hist = hist.shape[1]
    y = pool_mix(u_ext, n_hist, w_grp, scale) @ w_out
    return y, u_ext[:, -POOL_STATE:]


def sgu_spatial(v, w_s, b_s):
    B, L, _ = v.shape
    blk = min(L, SGU_BLOCK)
    nblk = L // blk
    vb = v.reshape(B, nblk, blk, SGU_HEADS, SGU_HEAD_DIM)
    cidx = jnp.arange(blk) // CHUNK
    mask = cidx[None, :] <= cidx[:, None]
    w = jnp.where(mask[None], w_s[:, :blk, :blk], 0.0)
    out = jnp.einsum('hij,bnjhd->bnihd', w, vb)
    out = out + jnp.transpose(b_s[:, :blk])[None, None, :, :, None]
    return out.reshape(B, L, D_SGU)


def sgu_mixer(h, w_in, ln_g, ln_b, w_s, b_s, w_out):
    z = jax.nn.gelu(h @ w_in)
    u, v = jnp.split(z, 2, axis=-1)
    v = layernorm(v, ln_g, ln_b)
    y = (u * sgu_spatial(v, w_s, b_s)) @ w_out
    return y, v


def swiglu(h, w_gate, w_up, w_down):
    return (jax.nn.silu(h @ w_gate) * (h @ w_up)) @ w_down


def moe_swiglu(h, router, w_gate, w_up, w_down):
    logits = (h @ router).astype(jnp.float32)
    top_val, top_idx = lax.top_k(logits, TOP_K)
    probs = jax.nn.softmax(top_val, axis=-1)
    gates = jnp.sum(jax.nn.one_hot(top_idx, N_EXPERTS, dtype=jnp.float32) * probs[..., None], axis=-2)
    gates = gates.astype(h.dtype)
    out = jnp.zeros(h.shape, h.dtype)
    for e in range(N_EXPERTS):
        out = out + gates[..., e:e + 1] * swiglu(h, w_gate[e], w_up[e], w_down[e])
    return out


def setup_inputs(seed: int = 0) -> dict:
    key = jax.random.key(seed)
    ks = iter(jax.random.split(key, 40))
    f32 = jnp.float32

    def nrm(shape, scale=1.0):
        return jax.random.normal(next(ks), shape, f32) * scale

    D = D_MODEL
    return {
        'x_prompt': nrm((BATCH, SEQ, D)),
        'x_sample': nrm((DEC_BATCH, DEC_SEQ, D)),
        'c_prompt': nrm((BATCH, D)),
        'c_sample': nrm((DEC_BATCH, D)),
        'state_pool': nrm((N_POOL_LAYERS, DEC_BATCH, POOL_STATE, D_POOL)),
        'ada_w': nrm((DEPTH, D, N_MOD * D), 0.5 * D ** -0.5),
        'ada_b': nrm((DEPTH, N_MOD * D), 0.1),
        'norm_g': 1.0 + nrm((DEPTH, 4, D), 0.1),
        'pool_w_in': nrm((N_POOL_LAYERS, D, D_POOL), D ** -0.5),
        'pool_w_grp': nrm((N_POOL_LAYERS, N_POOL_GROUPS, POOL_GROUP_DIM, POOL_GROUP_DIM), POOL_GROUP_DIM ** -0.5),
        'pool_scale': 1.0 + nrm((N_POOL_LAYERS, D_POOL), 0.1),
        'pool_w_out': nrm((N_POOL_LAYERS, D_POOL, D), D_POOL ** -0.5),
        'sgu_w_in': nrm((N_SGU_LAYERS, D, 2 * D_SGU), D ** -0.5),
        'sgu_ln_g': 1.0 + nrm((N_SGU_LAYERS, D_SGU), 0.1),
        'sgu_ln_b': nrm((N_SGU_LAYERS, D_SGU), 0.02),
        'sgu_w_s': nrm((N_SGU_LAYERS, SGU_HEADS, SGU_BLOCK, SGU_BLOCK), SGU_BLOCK ** -0.5),
        'sgu_b_s': 1.0 + nrm((N_SGU_LAYERS, SGU_HEADS, SGU_BLOCK), 0.1),
        'sgu_w_out': nrm((N_SGU_LAYERS, D_SGU, D), D_SGU ** -0.5),
        'ffn_w_gate': nrm((N_POOL_LAYERS, D, D_FF), D ** -0.5),
        'ffn_w_up': nrm((N_POOL_LAYERS, D, D_FF), D ** -0.5),
        'ffn_w_down': nrm((N_POOL_LAYERS, D_FF, D), D_FF ** -0.5),
        'moe_router': nrm((N_SGU_LAYERS, D, N_EXPERTS), D ** -0.5),
        'moe_w_gate': nrm((N_SGU_LAYERS, N_EXPERTS, D, D_EXPERT), D ** -0.5),
        'moe_w_up': nrm((N_SGU_LAYERS, N_EXPERTS, D, D_EXPERT), D ** -0.5),
        'moe_w_down': nrm((N_SGU_LAYERS, N_EXPERTS, D_EXPERT, D), D_EXPERT ** -0.5),
    }


def reference(x_prompt, x_sample, c_prompt, c_sample, state_pool, ada_w, ada_b, norm_g,
              pool_w_in, pool_w_grp, pool_scale, pool_w_out,
              sgu_w_in, sgu_ln_g, sgu_ln_b, sgu_w_s, sgu_b_s, sgu_w_out,
              ffn_w_gate, ffn_w_up, ffn_w_down,
              moe_router, moe_w_gate, moe_w_up, moe_w_down):
    xp, xs = x_prompt, x_sample
    pool_p, pool_s, sgu_s = [], [], []
    for l in range(DEPTH):
        j = l // N_MIXERS
        mp = adaln_mod(c_prompt, ada_w[l], ada_b[l])
        ms = adaln_mod(c_sample, ada_w[l], ada_b[l])
        hp = pre_norm(xp, norm_g[l, 0], mp[0], mp[1])
        hs = pre_norm(xs, norm_g[l, 0], ms[0], ms[1])
        if l % N_MIXERS == 0:
            mix = functools.partial(pool_mixer, w_in=pool_w_in[j], w_grp=pool_w_grp[j],
                                    scale=pool_scale[j], w_out=pool_w_out[j])
            yp, st_p = mix(hp, None)
            ys, st_s = mix(hs, state_pool[j])
            pool_p.append(st_p)
            pool_s.append(st_s)
        else:
            mix = functools.partial(sgu_mixer, w_in=sgu_w_in[j], ln_g=sgu_ln_g[j], ln_b=sgu_ln_b[j],
                                    w_s=sgu_w_s[j], b_s=sgu_b_s[j], w_out=sgu_w_out[j])
            yp, _ = mix(hp)
            ys, st_s = mix(hs)
            sgu_s.append(st_s)
        xp = post_add(xp, yp, norm_g[l, 1], mp[2])
        xs = post_add(xs, ys, norm_g[l, 1], ms[2])
        hp = pre_norm(xp, norm_g[l, 2], mp[3], mp[4])
        hs = pre_norm(xs, norm_g[l, 2], ms[3], ms[4])
        if l % 2 == 0:
            ffn = functools.partial(swiglu, w_gate=ffn_w_gate[j], w_up=ffn_w_up[j], w_down=ffn_w_down[j])
        else:
            ffn = functools.partial(moe_swiglu, router=moe_router[j], w_gate=moe_w_gate[j],
                                    w_up=moe_w_up[j], w_down=moe_w_down[j])
        xp = post_add(xp, ffn(hp), norm_g[l, 3], mp[5])
        xs = post_add(xs, ffn(hs), norm_g[l, 3], ms[5])
    new_pool_prompt = jnp.stack(pool_p, axis=0)
    new_pool_sample = jnp.stack(pool_s, axis=0)
    new_sgu_v_sample = jnp.stack(sgu_s, axis=0)
    return (xp, xs, new_pool_prompt, new_pool_sample, new_sgu_v_sample)
```

```python
import functools

import jax
import jax.numpy as jnp
from jax import lax
from jax.experimental import pallas as pl
from jax.experimental.pallas import tpu as pltpu

POOL_WINDOWS = (2, 4, 8, 16)
POOL_STATE = POOL_WINDOWS[-1] - 1
CHUNK = 64
SGU_BLOCK = 128
SGU_HEADS = 8
TOP_K = 2
N_MOD = 6
EPS = 1e-6

LANES = 128
SUBLANES = 8
HIST_ROWS = 16
VMEM_LIMIT_BYTES = 58 * 1024 * 1024

F32 = jnp.float32
BF16 = jnp.bfloat16


def _params(*sem):
    return pltpu.CompilerParams(dimension_semantics=sem, vmem_limit_bytes=VMEM_LIMIT_BYTES)


def _resident(shape):
    nd = len(shape)
    return pl.BlockSpec(shape, lambda *_: (0,) * nd, pipeline_mode=pl.Buffered(1))


def _pick(pref, n):
    b = min(pref, n)
    while n % b:
        b //= 2
    return b


def _rms(x, g):
    return x * lax.rsqrt(jnp.mean(x * x, axis=-1, keepdims=True) + EPS) * g


def _sigmoid(x):
    return 1.0 / (1.0 + jnp.exp(-x))


def _dot(a, b):
    return jnp.dot(a, b, preferred_element_type=F32)


def _ada_kernel(c_ref, w_ref, b_ref, o_ref):
    c = c_ref[...]
    a = (c * _sigmoid(c)).astype(BF16)
    o_ref[0] = _dot(a, w_ref[0].astype(BF16)) + b_ref[0]


def _ada(c_all, ada_w, ada_b):
    depth, d, n = ada_w.shape
    bb = c_all.shape[0]
    tn = _pick(1024, n)
    return pl.pallas_call(
        _ada_kernel,
        grid=(depth, n // tn),
        in_specs=[
            pl.BlockSpec((bb, d), lambda l, j: (0, 0)),
            pl.BlockSpec((1, d, tn), lambda l, j: (l, 0, j)),
            pl.BlockSpec((1, 1, tn), lambda l, j: (l, 0, j)),
        ],
        out_specs=pl.BlockSpec((1, bb, tn), lambda l, j: (l, 0, j)),
        out_shape=jax.ShapeDtypeStruct((depth, bb, n), F32),
        compiler_params=_params("arbitrary", "arbitrary"),
        name="ada",
    )(c_all, ada_w, ada_b.reshape(depth, 1, n))


def _front(x_ref, prev, mod, g_ref):
    x = x_ref[0]
    if prev is not None:
        y_ref, modp_ref, gp_ref = prev
        x = x + modp_ref[0, 5:6, :] * _rms(y_ref[...], gp_ref[3:4, :])
    h = _rms(x, g_ref[0:1, :]) * (1.0 + mod[1:2, :]) + mod[0:1, :]
    return x, h


def _back(x, y, mod, g_ref):
    x2 = x + mod[2:3, :] * _rms(y, g_ref[1:2, :])
    h2 = _rms(x2, g_ref[2:3, :]) * (1.0 + mod[4:5, :]) + mod[3:4, :]
    return x2, h2


def _pool_kernel(*refs, has_prev, has_hist, tm, n_blk, cg):
    it = iter(refs)
    x_ref = next(it)
    prev = (next(it), next(it), next(it)) if has_prev else None
    mod_ref, g_ref = next(it), next(it)
    hist_ref = next(it) if has_hist else None
    win_ref, wgrp_ref, scale_ref, wout_ref = next(it), next(it), next(it), next(it)
    x2_ref, h2_ref, st_ref = next(it), next(it), next(it)
    uext_ref = next(it)

    i = pl.program_id(1)
    mod = mod_ref[0]
    x, h = _front(x_ref, prev, mod, g_ref)
    u = _dot(h.astype(BF16), win_ref[...])

    @pl.when(i == 0)
    def _():
        if has_hist:
            uext_ref[0:HIST_ROWS, :] = hist_ref[0]
        else:
            uext_ref[0:HIST_ROWS, :] = jnp.zeros((HIST_ROWS, u.shape[1]), F32)

    @pl.when(i > 0)
    def _():
        uext_ref[0:HIST_ROWS, :] = uext_ref[tm:tm + HIST_ROWS, :]

    uext_ref[HIST_ROWS:, :] = u

    pos = i * tm + lax.broadcasted_iota(jnp.int32, (tm, 1), 0)
    if has_hist:
        pos = pos + POOL_STATE
    parts = []
    for gi, w in enumerate(POOL_WINDOWS):
        sl = slice(gi * cg, (gi + 1) * cg)
        s = uext_ref[HIST_ROWS:HIST_ROWS + tm, sl]
        for k in range(1, w):
            s = s + uext_ref[HIST_ROWS - k:HIST_ROWS - k + tm, sl]
        cnt = jnp.minimum(pos + 1, w).astype(F32)
        pooled = s / cnt - uext_ref[HIST_ROWS:HIST_ROWS + tm, sl]
        mixed = _dot(pooled.astype(BF16), wgrp_ref[gi]) * scale_ref[:, sl]
        parts.append(mixed.astype(BF16))
    y = _dot(jnp.concatenate(parts, axis=1), wout_ref[...])

    x2, h2 = _back(x, y, mod, g_ref)
    x2_ref[0] = x2
    h2_ref[...] = h2.astype(BF16)

    @pl.when(i == n_blk - 1)
    def _():
        st_ref[0] = uext_ref[tm:tm + HIST_ROWS, :]


def _pool_layer(x, prev, mod, g, hist, w_in, w_grp, scale, w_out, h2_all, row0):
    b, l, d = x.shape
    has_prev, has_hist = prev is not None, hist is not None
    tm = _pick(256, l)
    n_blk = l // tm
    blk0 = row0 // tm
    cg = d // len(POOL_WINDOWS)

    row_map = lambda bi, i: (blk0 + bi * n_blk + i, 0)
    in_specs = [pl.BlockSpec((1, tm, d), lambda bi, i: (bi, i, 0))]
    args = [x]
    if has_prev:
        y_all, mod_prev, g_prev = prev
        in_specs += [pl.BlockSpec((tm, d), row_map),
                     pl.BlockSpec((1, N_MOD, d), lambda bi, i: (bi, 0, 0)),
                     _resident(g_prev.shape)]
        args += [y_all, mod_prev, g_prev]
    in_specs += [pl.BlockSpec((1, N_MOD, d), lambda bi, i: (bi, 0, 0)), _resident(g.shape)]
    args += [mod, g]
    if has_hist:
        in_specs.append(pl.BlockSpec((1, HIST_ROWS, d), lambda bi, i: (bi, 0, 0)))
        args.append(hist)
    in_specs += [_resident(w_in.shape), _resident(w_grp.shape), _resident(scale.shape), _resident(w_out.shape)]
    args += [w_in, w_grp, scale, w_out]

    n_in = len(args)
    in_specs.append(pl.BlockSpec(memory_space=pl.ANY))
    args.append(h2_all)

    def kern(*refs):
        refs = refs[:n_in] + refs[n_in + 1:]
        _pool_kernel(*refs, has_prev=has_prev, has_hist=has_hist, tm=tm, n_blk=n_blk, cg=cg)

    return pl.pallas_call(
        kern,
        grid=(b, n_blk),
        in_specs=in_specs,
        out_specs=[pl.BlockSpec((1, tm, d), lambda bi, i: (bi, i, 0)),
                   pl.BlockSpec((tm, d), row_map),
                   pl.BlockSpec((1, HIST_ROWS, d), lambda bi, i: (bi, 0, 0))],
        out_shape=[jax.ShapeDtypeStruct((b, l, d), F32),
                   jax.ShapeDtypeStruct(h2_all.shape, h2_all.dtype),
                   jax.ShapeDtypeStruct((b, HIST_ROWS, d), F32)],
        scratch_shapes=[pltpu.VMEM((tm + HIST_ROWS, d), F32)],
        input_output_aliases={n_in: 1},
        compiler_params=_params("arbitrary", "arbitrary"),
        name="pool_hist" if has_hist else "pool",
    )(*args)


def _ffn_kernel(x_ref, wg_ref, wu_ref, wd_ref, o_ref, *, tn):
    j = pl.program_id(1)
    x = x_ref[...]
    gt = _dot(x, wg_ref[...])
    up = _dot(x, wu_ref[...])
    a = (gt * _sigmoid(gt) * up).astype(BF16)

    @pl.when(j == 0)
    def _():
        o_ref[...] = jnp.zeros(o_ref.shape, F32)

    for n in range(0, o_ref.shape[1], tn):
        o_ref[:, n:n + tn] += _dot(a, wd_ref[:, n:n + tn])


def _ffn(h_all, w_gate, w_up, w_down):
    t, d = h_all.shape
    f = w_gate.shape[1]
    tm, tf = _pick(1024, t), _pick(512, f)
    return pl.pallas_call(
        functools.partial(_ffn_kernel, tn=_pick(512, d)),
        grid=(t // tm, f // tf),
        in_specs=[pl.BlockSpec((tm, d), lambda i, j: (i, 0)),
                  pl.BlockSpec((d, tf), lambda i, j: (0, j)),
                  pl.BlockSpec((d, tf), lambda i, j: (0, j)),
                  pl.BlockSpec((tf, d), lambda i, j: (j, 0))],
        out_specs=pl.BlockSpec((tm, d), lambda i, j: (i, 0)),
        out_shape=jax.ShapeDtypeStruct((t, d), F32),
        compiler_params=_params("arbitrary", "arbitrary"),
        name="ffn",
    )(h_all, w_gate, w_up, w_down)


def _gelu_tanh(x):
    return 0.5 * x * (1.0 + jnp.tanh(0.7978845608028654 * (x + 0.044715 * (x * x * x))))


def _sgu_kernel(*refs, has_v, tm, n_exp):
    it = iter(refs)
    x_ref = next(it)
    prev = (next(it), next(it), next(it))
    mod_ref, g_ref = next(it), next(it)
    win_ref, lng_ref, lnb_ref, ws_ref, bs_ref, wout_ref, router_ref = (next(it) for _ in range(7))
    x2_ref, hp_ref, idx_ref, gate_ref = next(it), next(it), next(it), next(it)
    v_ref = next(it) if has_v else None

    mod = mod_ref[0]
    x, h = _front(x_ref, prev, mod, g_ref)
    z = _gelu_tanh(_dot(h.astype(BF16), win_ref[...]))
    d = z.shape[1] // 2
    u, v = z[:, :d], z[:, d:]
    vc = v - jnp.mean(v, axis=-1, keepdims=True)
    v = vc * lax.rsqrt(jnp.mean(vc * vc, axis=-1, keepdims=True) + EPS) * lng_ref[...] + lnb_ref[...]
    if has_v:
        v_ref[0] = v
    vb = v.astype(BF16)

    blk = min(tm, SGU_BLOCK)
    ci = lax.broadcasted_iota(jnp.int32, (blk, blk), 0) // CHUNK
    cj = lax.broadcasted_iota(jnp.int32, (blk, blk), 1) // CHUNK
    dh = d // SGU_HEADS
    wm = [jnp.where(cj <= ci, ws_ref[hd, 0:blk, 0:blk], 0.0).astype(BF16) for hd in range(SGU_HEADS)]
    rows = []
    for r in range(tm // blk):
        heads = []
        for hd in range(SGU_HEADS):
            o = _dot(wm[hd], vb[r * blk:(r + 1) * blk, hd * dh:(hd + 1) * dh])
            heads.append(o + bs_ref[0:blk, hd:hd + 1])
        rows.append(jnp.concatenate(heads, axis=1))
    sp = rows[0] if len(rows) == 1 else jnp.concatenate(rows, axis=0)
    y = _dot((u * sp).astype(BF16), wout_ref[...])

    x2, h2 = _back(x, y, mod, g_ref)
    x2_ref[0] = x2

    hb = h2.astype(BF16).astype(F32)
    lo = lax.bitcast_convert_type(hb[:, :d // 2], jnp.uint32) >> 16
    hi = lax.bitcast_convert_type(hb[:, d // 2:], jnp.uint32) & jnp.uint32(0xFFFF0000)
    hp_ref[...] = hi | lo

    logits = jnp.dot(h2, router_ref[...], preferred_element_type=F32, precision=lax.Precision.HIGHEST)
    lane = lax.broadcasted_iota(jnp.int32, logits.shape, 1)
    lg = jnp.where(lane < n_exp, logits, -jnp.inf)
    m1 = jnp.max(lg, axis=1, keepdims=True)
    i1 = jnp.min(jnp.where(lg == m1, lane, LANES), axis=1, keepdims=True)
    lg = jnp.where(lane == i1, -jnp.inf, lg)
    m2 = jnp.max(lg, axis=1, keepdims=True)
    i2 = jnp.min(jnp.where(lg == m2, lane, LANES), axis=1, keepdims=True)
    e = jnp.exp(m2 - m1)
    p1 = 1.0 / (1.0 + e)
    p2 = e / (1.0 + e)
    idx_ref[...] = jnp.where(lane == 0, i1, jnp.where(lane == 1, i2, 0))
    gate_ref[...] = jnp.where(lane == 0, p1, jnp.where(lane == 1, p2, 0.0))


def _sgu_layer(x, prev, mod, g, w_in, ln_g, ln_b, w_s, b_st, w_out, router_p, n_exp, bufs, row0, has_v):
    b, l, d = x.shape
    tm = _pick(256, l)
    n_blk = l // tm
    blk0 = row0 // tm
    y_all, mod_prev, g_prev = prev
    row_map = lambda bi, i: (blk0 + bi * n_blk + i, 0)
    mod_spec = pl.BlockSpec((1, N_MOD, d), lambda bi, i: (bi, 0, 0))

    in_specs = [pl.BlockSpec((1, tm, d), lambda bi, i: (bi, i, 0)),
                pl.BlockSpec((tm, d), row_map), mod_spec, _resident(g_prev.shape),
                mod_spec, _resident(g.shape)]
    args = [x, y_all, mod_prev, g_prev, mod, g]
    for wt in (w_in, ln_g, ln_b, w_s, b_st, w_out, router_p):
        in_specs.append(_resident(wt.shape))
        args.append(wt)

    out_specs = [pl.BlockSpec((1, tm, d), lambda bi, i: (bi, i, 0)),
                 pl.BlockSpec((tm, d // 2), row_map),
                 pl.BlockSpec((tm, LANES), row_map),
                 pl.BlockSpec((tm, LANES), row_map)]
    out_shape = [jax.ShapeDtypeStruct((b, l, d), F32)]
    out_shape += [jax.ShapeDtypeStruct(buf.shape, buf.dtype) for buf in bufs]
    if has_v:
        out_specs.append(pl.BlockSpec((1, tm, d), lambda bi, i: (bi, i, 0)))
        out_shape.append(jax.ShapeDtypeStruct((b, l, d), F32))

    aliases = {}
    n_in = len(args)
    for k, buf in enumerate(bufs):
        in_specs.append(pl.BlockSpec(memory_space=pl.ANY))
        args.append(buf)
        aliases[n_in + k] = 1 + k
    n_alias = len(aliases)

    def kern(*refs):
        refs = refs[:n_in] + refs[n_in + n_alias:]
        _sgu_kernel(*refs, has_v=has_v, tm=tm, n_exp=n_exp)

    return pl.pallas_call(
        kern,
        grid=(b, n_blk),
        in_specs=in_specs,
        out_specs=out_specs,
        out_shape=out_shape,
        input_output_aliases=aliases,
        compiler_params=_params("arbitrary", "arbitrary"),
        name="sgu_v" if has_v else "sgu",
    )(*args)


def _route_kernel(idx_ref, pos_ref, meta_ref, cnt_ref, carry_ref, start_ref, *, tb, tm_g, n_exp, nbl):
    p, i = pl.program_id(0), pl.program_id(1)
    lane = lax.broadcasted_iota(jnp.int32, (tb, LANES), 1)
    idx = idx_ref[...]
    oh1 = lane == idx[:, 0:1]
    oh2 = lane == idx[:, 1:2]
    m = jnp.where(oh1 | oh2, 1.0, 0.0)
    colsum = jnp.sum(m, axis=0, keepdims=True)

    @pl.when((p == 0) & (i == 0))
    def _():
        cnt_ref[...] = jnp.zeros((1, LANES), F32)

    @pl.when(p == 0)
    def _():
        cnt_ref[...] += colsum

    @pl.when((p == 1) & (i == 0))
    def _():
        cnt = cnt_ref[...]
        padded = jnp.ceil(cnt * (1.0 / tm_g)) * float(tm_g)
        lane1 = lax.broadcasted_iota(jnp.int32, (1, LANES), 1)
        blk_row = lax.broadcasted_iota(jnp.int32, (1, nbl), 1).astype(F32) * float(tm_g)
        run = jnp.zeros((1, 1), F32)
        starts = jnp.zeros((1, LANES), F32)
        ends = jnp.zeros((1, LANES), F32)
        blk_exp = jnp.zeros((1, nbl), F32)
        for ex in range(n_exp):
            starts = jnp.where(lane1 == ex, run, starts)
            run = run + jnp.sum(jnp.where(lane1 == ex, padded, 0.0), axis=1, keepdims=True)
            ends = jnp.where(lane1 == ex, run, ends)
            blk_exp = blk_exp + jnp.where(blk_row >= run, 1.0, 0.0)
        start_ref[...] = starts
        carry_ref[...] = jnp.zeros((1, LANES), F32)
        meta_ref[...] = jnp.zeros(meta_ref.shape, jnp.int32)
        meta_ref[0:1, :] = jnp.minimum(blk_exp, float(n_exp - 1)).astype(jnp.int32)
        meta_ref[1:2, :] = jnp.broadcast_to(run * (1.0 / tm_g), (1, nbl)).astype(jnp.int32)
        meta_ref[2:3, 0:LANES] = (starts + cnt).astype(jnp.int32)
        meta_ref[3:4, 0:LANES] = ends.astype(jnp.int32)

    @pl.when(p == 1)
    def _():
        r = lax.broadcasted_iota(jnp.int32, (tb, tb), 0)
        c = lax.broadcasted_iota(jnp.int32, (tb, tb), 1)
        tri = jnp.where(c < r, 1.0, 0.0).astype(BF16)
        rank = _dot(tri, m.astype(BF16)) + carry_ref[...]
        dest = start_ref[...] + rank
        p1 = jnp.sum(jnp.where(oh1, dest, 0.0), axis=1, keepdims=True)
        p2 = jnp.sum(jnp.where(oh2, dest, 0.0), axis=1, keepdims=True)
        pos_ref[...] = jnp.where(lane == 0, p1, jnp.where(lane == 1, p2, 0.0)).astype(jnp.int32)
        carry_ref[...] += colsum


def _route(idx_all, tm_g, n_exp, nbl):
    t = idx_all.shape[0]
    tb = _pick(512, t)
    return pl.pallas_call(
        functools.partial(_route_kernel, tb=tb, tm_g=tm_g, n_exp=n_exp, nbl=nbl),
        grid=(2, t // tb),
        in_specs=[pl.BlockSpec((tb, LANES), lambda p, i: (i, 0))],
        out_specs=[pl.BlockSpec((tb, LANES), lambda p, i: (p * i, 0)),
                   pl.BlockSpec((SUBLANES, nbl), lambda p, i: (0, 0))],
        out_shape=[jax.ShapeDtypeStruct((t, LANES), jnp.int32),
                   jax.ShapeDtypeStruct((SUBLANES, nbl), jnp.int32)],
        scratch_shapes=[pltpu.VMEM((1, LANES), F32)] * 3,
        compiler_params=_params("arbitrary", "arbitrary"),
        name="route",
    )(idx_all)


def _dispatch_kernel(lo_ref, hi_ref, nu_ref, pos0_ref, pos1_ref, h_ref, xs_ref, sem, *, tmd, n_exp, tm_g, n_blocks):
    i = pl.program_id(0)

    def row_copy(src_row, dst_row):
        return pltpu.make_async_copy(h_ref.at[pl.ds(src_row, 1)], xs_ref.at[pl.ds(dst_row, 1)], sem)

    @pl.when(i == 0)
    def _():
        def fill_block(b, c):
            cp = pltpu.make_async_copy(h_ref.at[pl.ds(0, tm_g)], xs_ref.at[pl.ds(b * tm_g, tm_g)], sem)
            cp.start()
            cp.wait()
            return c

        lax.fori_loop(nu_ref[0], n_blocks, fill_block, 0)

        for ex in range(n_exp):
            lo, hi = lo_ref[ex], hi_ref[ex]

            def fill(r, c):
                row_copy(0, r).start()
                return c

            def drain(r, c):
                row_copy(0, r).wait()
                return c

            lax.fori_loop(lo, hi, fill, 0)
            lax.fori_loop(lo, hi, drain, 0)

    base = i * tmd

    def issue(r, c):
        row_copy(base + r, pos0_ref[r]).start()
        row_copy(base + r, pos1_ref[r]).start()
        return c

    def drain2(r, c):
        row_copy(0, 0).wait()
        row_copy(0, 0).wait()
        return c

    lax.fori_loop(0, tmd, issue, 0)
    lax.fori_loop(0, tmd, drain2, 0)


def _dispatch(pad_lo, pad_hi, n_used, pos0, pos1, hp_all, tm_g, n_blocks, n_exp):
    t, dw = hp_all.shape
    assert t >= tm_g
    n_rows = n_blocks * tm_g
    tmd = _pick(1024, t)
    return pl.pallas_call(
        functools.partial(_dispatch_kernel, tmd=tmd, n_exp=n_exp, tm_g=tm_g, n_blocks=n_blocks),
        grid_spec=pltpu.PrefetchScalarGridSpec(
            num_scalar_prefetch=3,
            grid=(t // tmd,),
            in_specs=[pl.BlockSpec((tmd,), lambda i, *_: (i,), memory_space=pltpu.SMEM),
                      pl.BlockSpec((tmd,), lambda i, *_: (i,), memory_space=pltpu.SMEM),
                      pl.BlockSpec(memory_space=pl.ANY)],
            out_specs=pl.BlockSpec(memory_space=pl.ANY),
            scratch_shapes=[pltpu.SemaphoreType.DMA(())],
        ),
        out_shape=jax.ShapeDtypeStruct((n_rows, dw), hp_all.dtype),
        compiler_params=pltpu.CompilerParams(dimension_semantics=("arbitrary",), has_side_effects=True),
        name="dispatch",
    )(pad_lo, pad_hi, n_used, pos0, pos1, hp_all)


def _moe_kernel(be_ref, nu_ref, xs_ref, wg_ref, wu_ref, wd_ref, o_ref, xb_ref, *, tn):
    b, j = pl.program_id(0), pl.program_id(1)
    dw = xs_ref.shape[1]

    @pl.when(j == 0)
    def _():
        o_ref[...] = jnp.zeros(o_ref.shape, F32)

    @pl.when(b < nu_ref[0])
    def _():
        @pl.when(j == 0)
        def _():
            w = xs_ref[...]
            xb_ref[:, :dw] = lax.bitcast_convert_type(w << 16, F32).astype(BF16)
            xb_ref[:, dw:] = lax.bitcast_convert_type(w & jnp.uint32(0xFFFF0000), F32).astype(BF16)

        x = xb_ref[...]
        gt = _dot(x, wg_ref[0])
        up = _dot(x, wu_ref[0])
        a = (gt * _sigmoid(gt) * up).astype(BF16)
        for n in range(0, o_ref.shape[1], tn):
            o_ref[:, n:n + tn] += _dot(a, wd_ref[0, :, n:n + tn])


def _moe(blk_exp, n_used, xs, w_gate, w_up, w_down, tm_g):
    n_rows, dw = xs.shape
    _, d, f = w_gate.shape
    tf = _pick(512, f)
    n_j = f // tf

    def live(b, nu):
        return jnp.minimum(b, nu[0] - 1)

    def jj(b, j, nu):
        return jnp.where(b < nu[0], j, n_j - 1)

    return pl.pallas_call(
        functools.partial(_moe_kernel, tn=_pick(512, d)),
        grid_spec=pltpu.PrefetchScalarGridSpec(
            num_scalar_prefetch=2,
            grid=(n_rows // tm_g, n_j),
            in_specs=[pl.BlockSpec((tm_g, dw), lambda b, j, be, nu: (live(b, nu), 0)),
                      pl.BlockSpec((1, d, tf), lambda b, j, be, nu: (be[live(b, nu)], 0, jj(b, j, nu))),
                      pl.BlockSpec((1, d, tf), lambda b, j, be, nu: (be[live(b, nu)], 0, jj(b, j, nu))),
                      pl.BlockSpec((1, tf, d), lambda b, j, be, nu: (be[live(b, nu)], jj(b, j, nu), 0))],
            out_specs=pl.BlockSpec((tm_g, d), lambda b, j, be, nu: (b, 0)),
            scratch_shapes=[pltpu.VMEM((tm_g, d), BF16)],
        ),
        out_shape=jax.ShapeDtypeStruct((n_rows, d), F32),
        compiler_params=_params("arbitrary", "arbitrary"),
        name="moe",
    )(blk_exp, n_used, xs, w_gate, w_up, w_down)


def _combine_kernel(pos0_ref, pos1_ref, gate_ref, ys_ref, o_ref, buf_ref, sem, *, tmc):
    def row_copy(k, src_row, r):
        return pltpu.make_async_copy(ys_ref.at[pl.ds(src_row, 1)], buf_ref.at[k, pl.ds(r, 1)], sem)

    def issue(r, c):
        row_copy(0, pos0_ref[r], r).start()
        row_copy(1, pos1_ref[r], r).start()
        return c

    def drain(r, c):
        row_copy(0, 0, r).wait()
        row_copy(1, 0, r).wait()
        return c

    lax.fori_loop(0, tmc, issue, 0)
    lax.fori_loop(0, tmc, drain, 0)
    gate = gate_ref[...]
    o_ref[...] = gate[:, 0:1] * buf_ref[0] + gate[:, 1:2] * buf_ref[1]


def _combine(pos0, pos1, gate_all, ys):
    t = gate_all.shape[0]
    d = ys.shape[1]
    tmc = _pick(1024, t)
    return pl.pallas_call(
        functools.partial(_combine_kernel, tmc=tmc),
        grid=(t // tmc,),
        in_specs=[pl.BlockSpec((tmc,), lambda i: (i,), memory_space=pltpu.SMEM),
                  pl.BlockSpec((tmc,), lambda i: (i,), memory_space=pltpu.SMEM),
                  pl.BlockSpec((tmc, LANES), lambda i: (i, 0)),
                  pl.BlockSpec(memory_space=pl.ANY)],
        out_specs=pl.BlockSpec((tmc, d), lambda i: (i, 0)),
        out_shape=jax.ShapeDtypeStruct((t, d), F32),
        scratch_shapes=[pltpu.VMEM((TOP_K, tmc, d), F32), pltpu.SemaphoreType.DMA(())],
        compiler_params=_params("arbitrary"),
        name="combine",
    )(pos0, pos1, gate_all, ys)


def _final_kernel(x_ref, y_ref, mod_ref, g_ref, o_ref):
    o_ref[0] = x_ref[0] + mod_ref[0, 5:6, :] * _rms(y_ref[...], g_ref[3:4, :])


def _final(x, y_all, mod, g, row0):
    b, l, d = x.shape
    tm = _pick(512, l)
    n_blk = l // tm
    blk0 = row0 // tm
    return pl.pallas_call(
        _final_kernel,
        grid=(b, n_blk),
        in_specs=[pl.BlockSpec((1, tm, d), lambda bi, i: (bi, i, 0)),
                  pl.BlockSpec((tm, d), lambda bi, i: (blk0 + bi * n_blk + i, 0)),
                  pl.BlockSpec((1, N_MOD, d), lambda bi, i: (bi, 0, 0)),
                  _resident(g.shape)],
        out_specs=pl.BlockSpec((1, tm, d), lambda bi, i: (bi, i, 0)),
        out_shape=jax.ShapeDtypeStruct((b, l, d), F32),
        compiler_params=_params("arbitrary", "arbitrary"),
        name="final",
    )(x, y_all, mod, g)


def kernel(x_prompt, x_sample, c_prompt, c_sample, state_pool, ada_w, ada_b, norm_g, pool_w_in, pool_w_grp, pool_scale, pool_w_out, sgu_w_in, sgu_ln_g, sgu_ln_b, sgu_w_s, sgu_b_s, sgu_w_out, ffn_w_gate, ffn_w_up, ffn_w_down, moe_router, moe_w_gate, moe_w_up, moe_w_down):
    bp, lp, d = x_prompt.shape
    bs, ls, _ = x_sample.shape
    depth = ada_w.shape[0]
    n_exp = moe_router.shape[-1]
    tp, ts = bp * lp, bs * ls
    t_all = tp + ts
    assert n_exp <= LANES and state_pool.shape[2] == POOL_STATE and ls <= SGU_BLOCK

    tm_g = _pick(1024, TOP_K * t_all)
    n_blocks = -(-TOP_K * t_all // tm_g) + n_exp
    nbl = -(-n_blocks // LANES) * LANES

    mods = _ada(jnp.concatenate([c_prompt, c_sample], axis=0), ada_w, ada_b)
    mods = mods.reshape(depth, bp + bs, N_MOD, d)
    mod_p = [mods[l, :bp] for l in range(depth)]
    mod_s = [mods[l, bp:] for l in range(depth)]
    hist = jnp.pad(state_pool, ((0, 0), (0, 0), (HIST_ROWS - POOL_STATE, 0), (0, 0)))

    h_all = jnp.zeros((t_all, d), BF16)
    route_bufs = (jnp.zeros((t_all, d // 2), jnp.uint32), jnp.zeros((t_all, LANES), jnp.int32),
                  jnp.zeros((t_all, LANES), F32))

    xp, xs = x_prompt, x_sample
    y_all = None
    pool_p, pool_s, sgu_s = [], [], []
    for l in range(depth):
        jl = l // 2
        g = norm_g[l]
        prev_p = prev_s = None
        if l > 0:
            prev_p = (y_all, mod_p[l - 1], norm_g[l - 1])
            prev_s = (y_all, mod_s[l - 1], norm_g[l - 1])
        if l % 2 == 0:
            w = (pool_w_in[jl].astype(BF16), pool_w_grp[jl].astype(BF16),
                 pool_scale[jl].reshape(1, d), pool_w_out[jl].astype(BF16))
            xp, h_all, st_p = _pool_layer(xp, prev_p, mod_p[l], g, None, *w, h_all, 0)
            xs, h_all, st_s = _pool_layer(xs, prev_s, mod_s[l], g, hist[jl], *w, h_all, tp)
            pool_p.append(st_p[:, HIST_ROWS - POOL_STATE:])
            pool_s.append(st_s[:, HIST_ROWS - POOL_STATE:])
            y_all = _ffn(h_all, ffn_w_gate[jl].astype(BF16), ffn_w_up[jl].astype(BF16),
                         ffn_w_down[jl].astype(BF16))
        else:
            router_p = jnp.pad(moe_router[jl], ((0, 0), (0, LANES - n_exp)))
            w = (sgu_w_in[jl].astype(BF16), sgu_ln_g[jl].reshape(1, d), sgu_ln_b[jl].reshape(1, d),
                 sgu_w_s[jl], jnp.transpose(sgu_b_s[jl]), sgu_w_out[jl].astype(BF16), router_p, n_exp)
            xp, *route_bufs = _sgu_layer(xp, prev_p, mod_p[l], g, *w, route_bufs, 0, False)
            xs, *route_bufs, v_s = _sgu_layer(xs, prev_s, mod_s[l], g, *w, route_bufs, tp, True)
            hp_all, idx_all, gate_all = route_bufs
            sgu_s.append(v_s)
            pos_all, meta = _route(idx_all, tm_g, n_exp, nbl)
            pos0, pos1 = pos_all[:, 0], pos_all[:, 1]
            n_used = meta[1, :1]
            xsort = _dispatch(meta[2, :n_exp], meta[3, :n_exp], n_used, pos0, pos1, hp_all, tm_g, n_blocks, n_exp)
            ys = _moe(meta[0, :n_blocks], n_used, xsort, moe_w_gate[jl].astype(BF16),
                      moe_w_up[jl].astype(BF16), moe_w_down[jl].astype(BF16), tm_g)
            y_all = _combine(pos0, pos1, gate_all, ys)
    xp = _final(xp, y_all, mod_p[depth - 1], norm_g[depth - 1], 0)
    xs = _final(xs, y_all, mod_s[depth - 1], norm_g[depth - 1], tp)
    return (xp, xs, jnp.stack(pool_p, axis=0), jnp.stack(pool_s, axis=0), jnp.stack(sgu_s, axis=0))
```

```python
import functools

import jax
import jax.numpy as jnp
from jax import lax
from jax.experimental import pallas as pl
from jax.experimental.pallas import tpu as pltpu

POOL_WINDOWS = (2, 4, 8, 16)
POOL_STATE = POOL_WINDOWS[-1] - 1
CHUNK = 64
SGU_BLOCK = 128
SGU_HEADS = 8
TOP_K = 2
N_MOD = 6
EPS = 1e-6

LANES = 128
SUBLANES = 8
HIST_ROWS = 16
VMEM_LIMIT_BYTES = 58 * 1024 * 1024

F32 = jnp.float32
BF16 = jnp.bfloat16


def _params(*sem):
    return pltpu.CompilerParams(dimension_semantics=sem, vmem_limit_bytes=VMEM_LIMIT_BYTES)


def _resident(shape):
    nd = len(shape)
    return pl.BlockSpec(shape, lambda *_: (0,) * nd, pipeline_mode=pl.Buffered(1))


def _pick(pref, n):
    b = min(pref, n)
    while n % b:
        b //= 2
    return b


def _rms(x, g):
    return x * lax.rsqrt(jnp.mean(x * x, axis=-1, keepdims=True) + EPS) * g


def _sigmoid(x):
    return 1.0 / (1.0 + jnp.exp(-x))


def _dot(a, b):
    return jnp.dot(a, b, preferred_element_type=F32)


def _ada_kernel(c_ref, w_ref, b_ref, o_ref):
    c = c_ref[...]
    a = (c * _sigmoid(c)).astype(BF16)
    o_ref[0] = _dot(a, w_ref[0].astype(BF16)) + b_ref[0]


def _ada(c_all, ada_w, ada_b):
    depth, d, n = ada_w.shape
    bb = c_all.shape[0]
    tn = _pick(1024, n)
    return pl.pallas_call(
        _ada_kernel,
        grid=(depth, n // tn),
        in_specs=[
            pl.BlockSpec((bb, d), lambda l, j: (0, 0)),
            pl.BlockSpec((1, d, tn), lambda l, j: (l, 0, j)),
            pl.BlockSpec((1, 1, tn), lambda l, j: (l, 0, j)),
        ],
        out_specs=pl.BlockSpec((1, bb, tn), lambda l, j: (l, 0, j)),
        out_shape=jax.ShapeDtypeStruct((depth, bb, n), F32),
        compiler_params=_params("arbitrary", "arbitrary"),
        name="ada",
    )(c_all, ada_w, ada_b.reshape(depth, 1, n))


def _front(x_ref, prev, mod, g_ref):
    x = x_ref[0]
    if prev is not None:
        y_ref, modp_ref, gp_ref = prev
        x = x + modp_ref[0, 5:6, :] * _rms(y_ref[...], gp_ref[3:4, :])
    h = _rms(x, g_ref[0:1, :]) * (1.0 + mod[1:2, :]) + mod[0:1, :]
    return x, h


def _back(x, y, mod, g_ref):
    x2 = x + mod[2:3, :] * _rms(y, g_ref[1:2, :])
    h2 = _rms(x2, g_ref[2:3, :]) * (1.0 + mod[4:5, :]) + mod[3:4, :]
    return x2, h2


def _pool_kernel(*refs, has_prev, has_hist, tm, n_blk, cg):
    it = iter(refs)
    x_ref = next(it)
    prev = (next(it), next(it), next(it)) if has_prev else None
    mod_ref, g_ref = next(it), next(it)
    hist_ref = next(it) if has_hist else None
    win_ref, wgrp_ref, scale_ref, wout_ref = next(it), next(it), next(it), next(it)
    x2_ref, h2_ref, st_ref = next(it), next(it), next(it)
    uext_ref = next(it)

    i = pl.program_id(1)
    mod = mod_ref[0]
    x, h = _front(x_ref, prev, mod, g_ref)
    u = _dot(h.astype(BF16), win_ref[...])

    @pl.when(i == 0)
    def _():
        if has_hist:
            uext_ref[0:HIST_ROWS, :] = hist_ref[0]
        else:
            uext_ref[0:HIST_ROWS, :] = jnp.zeros((HIST_ROWS, u.shape[1]), F32)

    @pl.when(i > 0)
    def _():
        uext_ref[0:HIST_ROWS, :] = uext_ref[tm:tm + HIST_ROWS, :]

    uext_ref[HIST_ROWS:, :] = u

    pos = i * tm + lax.broadcasted_iota(jnp.int32, (tm, 1), 0)
    if has_hist:
        pos = pos + POOL_STATE
    parts = []
    for gi, w in enumerate(POOL_WINDOWS):
        sl = slice(gi * cg, (gi + 1) * cg)
        s = uext_ref[HIST_ROWS:HIST_ROWS + tm, sl]
        for k in range(1, w):
            s = s + uext_ref[HIST_ROWS - k:HIST_ROWS - k + tm, sl]
        cnt = jnp.minimum(pos + 1, w).astype(F32)
        pooled = s / cnt - uext_ref[HIST_ROWS:HIST_ROWS + tm, sl]
        mixed = _dot(pooled.astype(BF16), wgrp_ref[gi]) * scale_ref[:, sl]
        parts.append(mixed.astype(BF16))
    y = _dot(jnp.concatenate(parts, axis=1), wout_ref[...])

    x2, h2 = _back(x, y, mod, g_ref)
    x2_ref[0] = x2
    h2_ref[...] = h2.astype(BF16)

    @pl.when(i == n_blk - 1)
    def _():
        st_ref[0] = uext_ref[tm:tm + HIST_ROWS, :]


def _pool_layer(x, prev, mod, g, hist, w_in, w_grp, scale, w_out, h2_all, row0):
    b, l, d = x.shape
    has_prev, has_hist = prev is not None, hist is not None
    tm = _pick(256, l)
    n_blk = l // tm
    blk0 = row0 // tm
    cg = d // len(POOL_WINDOWS)

    row_map = lambda bi, i: (blk0 + bi * n_blk + i, 0)
    in_specs = [pl.BlockSpec((1, tm, d), lambda bi, i: (bi, i, 0))]
    args = [x]
    if has_prev:
        y_all, mod_prev, g_prev = prev
        in_specs += [pl.BlockSpec((tm, d), row_map),
                     pl.BlockSpec((1, N_MOD, d), lambda bi, i: (bi, 0, 0)),
                     _resident(g_prev.shape)]
        args += [y_all, mod_prev, g_prev]
    in_specs += [pl.BlockSpec((1, N_MOD, d), lambda bi, i: (bi, 0, 0)), _resident(g.shape)]
    args += [mod, g]
    if has_hist:
        in_specs.append(pl.BlockSpec((1, HIST_ROWS, d), lambda bi, i: (bi, 0, 0)))
        args.append(hist)
    in_specs += [_resident(w_in.shape), _resident(w_grp.shape), _resident(scale.shape), _resident(w_out.shape)]
    args += [w_in, w_grp, scale, w_out]

    n_in = len(args)
    in_specs.append(pl.BlockSpec(memory_space=pl.ANY))
    args.append(h2_all)

    def kern(*refs):
        refs = refs[:n_in] + refs[n_in + 1:]
        _pool_kernel(*refs, has_prev=has_prev, has_hist=has_hist, tm=tm, n_blk=n_blk, cg=cg)

    return pl.pallas_call(
        kern,
        grid=(b, n_blk),
        in_specs=in_specs,
        out_specs=[pl.BlockSpec((1, tm, d), lambda bi, i: (bi, i, 0)),
                   pl.BlockSpec((tm, d), row_map),
                   pl.BlockSpec((1, HIST_ROWS, d), lambda bi, i: (bi, 0, 0))],
        out_shape=[jax.ShapeDtypeStruct((b, l, d), F32),
                   jax.ShapeDtypeStruct(h2_all.shape, h2_all.dtype),
                   jax.ShapeDtypeStruct((b, HIST_ROWS, d), F32)],
        scratch_shapes=[pltpu.VMEM((tm + HIST_ROWS, d), F32)],
        input_output_aliases={n_in: 1},
        compiler_params=_params("arbitrary", "arbitrary"),
        name="pool_hist" if has_hist else "pool",
    )(*args)


def _ffn_kernel(x_ref, wg_ref, wu_ref, wd_ref, o_ref, *, tn):
    j = pl.program_id(1)
    x = x_ref[...]
    gt = _dot(x, wg_ref[...])
    up = _dot(x, wu_ref[...])
    a = (gt * _sigmoid(gt) * up).astype(BF16)

    @pl.when(j == 0)
    def _():
        o_ref[...] = jnp.zeros(o_ref.shape, F32)

    for n in range(0, o_ref.shape[1], tn):
        o_ref[:, n:n + tn] += _dot(a, wd_ref[:, n:n + tn])


def _ffn(h_all, w_gate, w_up, w_down, jl):
    t, d = h_all.shape
    f = w_gate.shape[2]
    tm, tf = _pick(1024, t), _pick(512, f)
    return pl.pallas_call(
        functools.partial(_ffn_kernel, tn=_pick(512, d)),
        grid=(t // tm, f // tf),
        in_specs=[pl.BlockSpec((tm, d), lambda i, j: (i, 0)),
                  pl.BlockSpec((None, d, tf), lambda i, j: (jl, 0, j)),
                  pl.BlockSpec((None, d, tf), lambda i, j: (jl, 0, j)),
                  pl.BlockSpec((None, tf, d), lambda i, j: (jl, j, 0))],
        out_specs=pl.BlockSpec((tm, d), lambda i, j: (i, 0)),
        out_shape=jax.ShapeDtypeStruct((t, d), F32),
        compiler_params=_params("arbitrary", "arbitrary"),
        name="ffn",
    )(h_all, w_gate, w_up, w_down)


def _gelu_tanh(x):
    return 0.5 * x * (1.0 + jnp.tanh(0.7978845608028654 * (x + 0.044715 * (x * x * x))))


def _sgu_kernel(*refs, has_v, tm, n_exp):
    it = iter(refs)
    x_ref = next(it)
    prev = (next(it), next(it), next(it))
    mod_ref, g_ref = next(it), next(it)
    win_ref, lng_ref, lnb_ref, ws_ref, bs_ref, wout_ref, router_ref = (next(it) for _ in range(7))
    x2_ref, hp_ref, idx_ref, gate_ref = next(it), next(it), next(it), next(it)
    v_ref = next(it) if has_v else None

    mod = mod_ref[0]
    x, h = _front(x_ref, prev, mod, g_ref)
    z = _gelu_tanh(_dot(h.astype(BF16), win_ref[...]))
    d = z.shape[1] // 2
    u, v = z[:, :d], z[:, d:]
    vc = v - jnp.mean(v, axis=-1, keepdims=True)
    v = vc * lax.rsqrt(jnp.mean(vc * vc, axis=-1, keepdims=True) + EPS) * lng_ref[...] + lnb_ref[...]
    if has_v:
        v_ref[0] = v
    vb = v.astype(BF16)

    blk = min(tm, SGU_BLOCK)
    ci = lax.broadcasted_iota(jnp.int32, (blk, blk), 0) // CHUNK
    cj = lax.broadcasted_iota(jnp.int32, (blk, blk), 1) // CHUNK
    dh = d // SGU_HEADS
    wm = [jnp.where(cj <= ci, ws_ref[hd, 0:blk, 0:blk], 0.0).astype(BF16) for hd in range(SGU_HEADS)]
    rows = []
    for r in range(tm // blk):
        heads = []
        for hd in range(SGU_HEADS):
            o = _dot(wm[hd], vb[r * blk:(r + 1) * blk, hd * dh:(hd + 1) * dh])
            heads.append(o + bs_ref[0:blk, hd:hd + 1])
        rows.append(jnp.concatenate(heads, axis=1))
    sp = rows[0] if len(rows) == 1 else jnp.concatenate(rows, axis=0)
    y = _dot((u * sp).astype(BF16), wout_ref[...])

    x2, h2 = _back(x, y, mod, g_ref)
    x2_ref[0] = x2

    hb16 = h2.astype(BF16)
    hb = hb16.astype(F32)
    lo = lax.bitcast_convert_type(hb[:, :d // 2], jnp.uint32) >> 16
    hi = lax.bitcast_convert_type(hb[:, d // 2:], jnp.uint32) & jnp.uint32(0xFFFF0000)
    hp_ref[...] = hi | lo

    rest = (h2 - hb).astype(BF16)
    pr = _dot(jnp.concatenate([hb16, rest], axis=0), router_ref[...])
    pr = pr[:tm] + pr[tm:]
    logits = pr + pltpu.roll(pr, LANES - n_exp, 1)
    lane = lax.broadcasted_iota(jnp.int32, logits.shape, 1)
    lg = jnp.where(lane < n_exp, logits, -jnp.inf)
    m1 = jnp.max(lg, axis=1, keepdims=True)
    i1 = jnp.min(jnp.where(lg == m1, lane, LANES), axis=1, keepdims=True)
    lg = jnp.where(lane == i1, -jnp.inf, lg)
    m2 = jnp.max(lg, axis=1, keepdims=True)
    i2 = jnp.min(jnp.where(lg == m2, lane, LANES), axis=1, keepdims=True)
    e = jnp.exp(m2 - m1)
    p1 = 1.0 / (1.0 + e)
    p2 = e / (1.0 + e)
    idx_ref[...] = jnp.where(lane == 0, i1, jnp.where(lane == 1, i2, 0))
    gate_ref[...] = jnp.where(lane == 0, p1, jnp.where(lane == 1, p2, 0.0))


def _sgu_layer(x, prev, mod, g, w_in, ln_g, ln_b, w_s, b_st, w_out, router_p, n_exp, bufs, row0, has_v):
    b, l, d = x.shape
    tm = _pick(256, l)
    n_blk = l // tm
    blk0 = row0 // tm
    y_all, mod_prev, g_prev = prev
    row_map = lambda bi, i: (blk0 + bi * n_blk + i, 0)
    mod_spec = pl.BlockSpec((1, N_MOD, d), lambda bi, i: (bi, 0, 0))

    in_specs = [pl.BlockSpec((1, tm, d), lambda bi, i: (bi, i, 0)),
                pl.BlockSpec((tm, d), row_map), mod_spec, _resident(g_prev.shape),
                mod_spec, _resident(g.shape)]
    args = [x, y_all, mod_prev, g_prev, mod, g]
    for wt in (w_in, ln_g, ln_b, w_s, b_st, w_out, router_p):
        in_specs.append(_resident(wt.shape))
        args.append(wt)

    out_specs = [pl.BlockSpec((1, tm, d), lambda bi, i: (bi, i, 0)),
                 pl.BlockSpec((tm, d // 2), row_map),
                 pl.BlockSpec((tm, LANES), row_map),
                 pl.BlockSpec((tm, LANES), row_map)]
    out_shape = [jax.ShapeDtypeStruct((b, l, d), F32)]
    out_shape += [jax.ShapeDtypeStruct(buf.shape, buf.dtype) for buf in bufs]
    if has_v:
        out_specs.append(pl.BlockSpec((1, tm, d), lambda bi, i: (bi, i, 0)))
        out_shape.append(jax.ShapeDtypeStruct((b, l, d), F32))

    aliases = {}
    n_in = len(args)
    for k, buf in enumerate(bufs):
        in_specs.append(pl.BlockSpec(memory_space=pl.ANY))
        args.append(buf)
        aliases[n_in + k] = 1 + k
    n_alias = len(aliases)

    def kern(*refs):
        refs = refs[:n_in] + refs[n_in + n_alias:]
        _sgu_kernel(*refs, has_v=has_v, tm=tm, n_exp=n_exp)

    return pl.pallas_call(
        kern,
        grid=(b, n_blk),
        in_specs=in_specs,
        out_specs=out_specs,
        out_shape=out_shape,
        input_output_aliases=aliases,
        compiler_params=_params("arbitrary", "arbitrary"),
        name="sgu_v" if has_v else "sgu",
    )(*args)


def _route_kernel(idx_ref, pos_ref, meta_ref, cnt_ref, carry_ref, start_ref, *, tb, tm_g, n_exp, nbl):
    p, i = pl.program_id(0), pl.program_id(1)
    lane = lax.broadcasted_iota(jnp.int32, (tb, LANES), 1)
    idx = idx_ref[...]
    oh1 = lane == idx[:, 0:1]
    oh2 = lane == idx[:, 1:2]
    m = jnp.where(oh1 | oh2, 1.0, 0.0)
    colsum = jnp.sum(m, axis=0, keepdims=True)

    @pl.when((p == 0) & (i == 0))
    def _():
        cnt_ref[...] = jnp.zeros((1, LANES), F32)

    @pl.when(p == 0)
    def _():
        cnt_ref[...] += colsum

    @pl.when((p == 1) & (i == 0))
    def _():
        cnt = cnt_ref[...]
        padded = jnp.ceil(cnt * (1.0 / tm_g)) * float(tm_g)
        lane1 = lax.broadcasted_iota(jnp.int32, (1, LANES), 1)
        blk_row = lax.broadcasted_iota(jnp.int32, (1, nbl), 1).astype(F32) * float(tm_g)
        run = jnp.zeros((1, 1), F32)
        starts = jnp.zeros((1, LANES), F32)
        ends = jnp.zeros((1, LANES), F32)
        blk_exp = jnp.zeros((1, nbl), F32)
        for ex in range(n_exp):
            starts = jnp.where(lane1 == ex, run, starts)
            run = run + jnp.sum(jnp.where(lane1 == ex, padded, 0.0), axis=1, keepdims=True)
            ends = jnp.where(lane1 == ex, run, ends)
            blk_exp = blk_exp + jnp.where(blk_row >= run, 1.0, 0.0)
        start_ref[...] = starts
        carry_ref[...] = jnp.zeros((1, LANES), F32)
        meta_ref[...] = jnp.zeros(meta_ref.shape, jnp.int32)
        meta_ref[0:1, :] = jnp.minimum(blk_exp, float(n_exp - 1)).astype(jnp.int32)
        meta_ref[1:2, :] = jnp.broadcast_to(run * (1.0 / tm_g), (1, nbl)).astype(jnp.int32)
        meta_ref[2:3, 0:LANES] = (starts + cnt).astype(jnp.int32)
        meta_ref[3:4, 0:LANES] = ends.astype(jnp.int32)

    @pl.when(p == 1)
    def _():
        r = lax.broadcasted_iota(jnp.int32, (tb, tb), 0)
        c = lax.broadcasted_iota(jnp.int32, (tb, tb), 1)
        tri = jnp.where(c < r, 1.0, 0.0).astype(BF16)
        rank = _dot(tri, m.astype(BF16)) + carry_ref[...]
        dest = start_ref[...] + rank
        p1 = jnp.sum(jnp.where(oh1, dest, 0.0), axis=1, keepdims=True)
        p2 = jnp.sum(jnp.where(oh2, dest, 0.0), axis=1, keepdims=True)
        pos_ref[...] = jnp.where(lane == 0, p1, jnp.where(lane == 1, p2, 0.0)).astype(jnp.int32)
        carry_ref[...] += colsum


def _route(idx_all, tm_g, n_exp, nbl):
    t = idx_all.shape[0]
    tb = _pick(512, t)
    return pl.pallas_call(
        functools.partial(_route_kernel, tb=tb, tm_g=tm_g, n_exp=n_exp, nbl=nbl),
        grid=(2, t // tb),
        in_specs=[pl.BlockSpec((tb, LANES), lambda p, i: (i, 0))],
        out_specs=[pl.BlockSpec((tb, LANES), lambda p, i: (p * i, 0)),
                   pl.BlockSpec((SUBLANES, nbl), lambda p, i: (0, 0))],
        out_shape=[jax.ShapeDtypeStruct((t, LANES), jnp.int32),
                   jax.ShapeDtypeStruct((SUBLANES, nbl), jnp.int32)],
        scratch_shapes=[pltpu.VMEM((1, LANES), F32)] * 3,
        compiler_params=_params("arbitrary", "arbitrary"),
        name="route",
    )(idx_all)


def _dispatch_kernel(lo_ref, hi_ref, nu_ref, pos0_ref, pos1_ref, h_ref, xs_ref, sem, *, tmd, n_exp, tm_g, n_blocks):
    i = pl.program_id(0)

    def row_copy(src_row, dst_row):
        return pltpu.make_async_copy(h_ref.at[pl.ds(src_row, 1)], xs_ref.at[pl.ds(dst_row, 1)], sem)

    def rows_wait(n):
        pltpu.make_async_copy(h_ref.at[pl.ds(0, n)], xs_ref.at[pl.ds(0, n)], sem).wait()

    @pl.when(i == 0)
    def _():
        def fill_block(b, c):
            cp = pltpu.make_async_copy(h_ref.at[pl.ds(0, tm_g)], xs_ref.at[pl.ds(b * tm_g, tm_g)], sem)
            cp.start()
            cp.wait()
            return c

        lax.fori_loop(nu_ref[0], n_blocks, fill_block, 0)

        for ex in range(n_exp):
            lo, hi = lo_ref[ex], hi_ref[ex]

            def fill(r, c):
                row_copy(0, r).start()
                return c

            def drain(r, c):
                row_copy(0, r).wait()
                return c

            lax.fori_loop(lo, hi, fill, 0)
            lax.fori_loop(lo, hi, drain, 0)

    def issue(r, c):
        row_copy(r, pos0_ref[r]).start()
        row_copy(r, pos1_ref[r]).start()
        return c

    lax.fori_loop(0, tmd, issue, 0)
    for _ in range(TOP_K):
        rows_wait(tmd)


def _dispatch(pad_lo, pad_hi, n_used, pos0, pos1, hp_all, tm_g, n_blocks, n_exp):
    t, dw = hp_all.shape
    n_rows = n_blocks * tm_g
    tmd = _pick(1024, t)
    assert tmd >= tm_g
    return pl.pallas_call(
        functools.partial(_dispatch_kernel, tmd=tmd, n_exp=n_exp, tm_g=tm_g, n_blocks=n_blocks),
        grid_spec=pltpu.PrefetchScalarGridSpec(
            num_scalar_prefetch=3,
            grid=(t // tmd,),
            in_specs=[pl.BlockSpec((tmd,), lambda i, *_: (i,), memory_space=pltpu.SMEM),
                      pl.BlockSpec((tmd,), lambda i, *_: (i,), memory_space=pltpu.SMEM),
                      pl.BlockSpec((tmd, dw), lambda i, *_: (i, 0))],
            out_specs=pl.BlockSpec(memory_space=pl.ANY),
            scratch_shapes=[pltpu.SemaphoreType.DMA(())],
        ),
        out_shape=jax.ShapeDtypeStruct((n_rows, dw), hp_all.dtype),
        compiler_params=pltpu.CompilerParams(dimension_semantics=("arbitrary",), has_side_effects=True),
        name="dispatch",
    )(pad_lo, pad_hi, n_used, pos0, pos1, hp_all)


def _moe_kernel(be_ref, nu_ref, xs_ref, wg_ref, wu_ref, wd_ref, o_ref, xb_ref, *, tn):
    b, j = pl.program_id(0), pl.program_id(1)
    dw = xs_ref.shape[1]

    @pl.when(j == 0)
    def _():
        o_ref[...] = jnp.zeros(o_ref.shape, F32)

    @pl.when(b < nu_ref[0])
    def _():
        @pl.when(j == 0)
        def _():
            w = xs_ref[...]
            xb_ref[:, :dw] = lax.bitcast_convert_type(w << 16, F32).astype(BF16)
            xb_ref[:, dw:] = lax.bitcast_convert_type(w & jnp.uint32(0xFFFF0000), F32).astype(BF16)

        x = xb_ref[...]
        gt = _dot(x, wg_ref[...])
        up = _dot(x, wu_ref[...])
        a = (gt * _sigmoid(gt) * up).astype(BF16)
        for n in range(0, o_ref.shape[1], tn):
            o_ref[:, n:n + tn] += _dot(a, wd_ref[:, n:n + tn])


def _moe(blk_exp, n_used, xs, w_gate, w_up, w_down, jl, tm_g):
    n_rows, dw = xs.shape
    _, _, d, f = w_gate.shape
    tf = _pick(512, f)
    n_j = f // tf

    def live(b, nu):
        return jnp.minimum(b, nu[0] - 1)

    def jj(b, j, nu):
        return jnp.where(b < nu[0], j, n_j - 1)

    return pl.pallas_call(
        functools.partial(_moe_kernel, tn=_pick(512, d)),
        grid_spec=pltpu.PrefetchScalarGridSpec(
            num_scalar_prefetch=2,
            grid=(n_rows // tm_g, n_j),
            in_specs=[pl.BlockSpec((tm_g, dw), lambda b, j, be, nu: (live(b, nu), 0)),
                      pl.BlockSpec((None, None, d, tf), lambda b, j, be, nu: (jl, be[live(b, nu)], 0, jj(b, j, nu))),
                      pl.BlockSpec((None, None, d, tf), lambda b, j, be, nu: (jl, be[live(b, nu)], 0, jj(b, j, nu))),
                      pl.BlockSpec((None, None, tf, d), lambda b, j, be, nu: (jl, be[live(b, nu)], jj(b, j, nu), 0))],
            out_specs=pl.BlockSpec((tm_g, d), lambda b, j, be, nu: (b, 0)),
            scratch_shapes=[pltpu.VMEM((tm_g, d), BF16)],
        ),
        out_shape=jax.ShapeDtypeStruct((n_rows, d), F32),
        compiler_params=_params("arbitrary", "arbitrary"),
        name="moe",
    )(blk_exp, n_used, xs, w_gate, w_up, w_down)


def _combine_kernel(pos0_ref, pos1_ref, gate_ref, ys_ref, o_ref, buf_ref, sem, *, tmc):
    def row_copy(k, src_row, r):
        return pltpu.make_async_copy(ys_ref.at[pl.ds(src_row, 1)], buf_ref.at[k, pl.ds(r, 1)], sem)

    def issue(r, c):
        row_copy(0, pos0_ref[r], r).start()
        row_copy(1, pos1_ref[r], r).start()
        return c

    lax.fori_loop(0, tmc, issue, 0)
    for k in range(TOP_K):
        pltpu.make_async_copy(ys_ref.at[pl.ds(0, tmc)], buf_ref.at[k], sem).wait()
    gate = gate_ref[...]
    o_ref[...] = gate[:, 0:1] * buf_ref[0] + gate[:, 1:2] * buf_ref[1]


def _combine(pos0, pos1, gate_all, ys):
    t = gate_all.shape[0]
    d = ys.shape[1]
    tmc = _pick(1024, t)
    return pl.pallas_call(
        functools.partial(_combine_kernel, tmc=tmc),
        grid=(t // tmc,),
        in_specs=[pl.BlockSpec((tmc,), lambda i: (i,), memory_space=pltpu.SMEM),
                  pl.BlockSpec((tmc,), lambda i: (i,), memory_space=pltpu.SMEM),
                  pl.BlockSpec((tmc, LANES), lambda i: (i, 0)),
                  pl.BlockSpec(memory_space=pl.ANY)],
        out_specs=pl.BlockSpec((tmc, d), lambda i: (i, 0)),
        out_shape=jax.ShapeDtypeStruct((t, d), F32),
        scratch_shapes=[pltpu.VMEM((TOP_K, tmc, d), F32), pltpu.SemaphoreType.DMA(())],
        compiler_params=_params("arbitrary"),
        name="combine",
    )(pos0, pos1, gate_all, ys)


def _final_kernel(x_ref, y_ref, mod_ref, g_ref, o_ref):
    o_ref[0] = x_ref[0] + mod_ref[0, 5:6, :] * _rms(y_ref[...], g_ref[3:4, :])


def _final(x, y_all, mod, g, row0):
    b, l, d = x.shape
    tm = _pick(512, l)
    n_blk = l // tm
    blk0 = row0 // tm
    return pl.pallas_call(
        _final_kernel,
        grid=(b, n_blk),
        in_specs=[pl.BlockSpec((1, tm, d), lambda bi, i: (bi, i, 0)),
                  pl.BlockSpec((tm, d), lambda bi, i: (blk0 + bi * n_blk + i, 0)),
                  pl.BlockSpec((1, N_MOD, d), lambda bi, i: (bi, 0, 0)),
                  _resident(g.shape)],
        out_specs=pl.BlockSpec((1, tm, d), lambda bi, i: (bi, i, 0)),
        out_shape=jax.ShapeDtypeStruct((b, l, d), F32),
        compiler_params=_params("arbitrary", "arbitrary"),
        name="final",
    )(x, y_all, mod, g)


def kernel(x_prompt, x_sample, c_prompt, c_sample, state_pool, ada_w, ada_b, norm_g, pool_w_in, pool_w_grp, pool_scale, pool_w_out, sgu_w_in, sgu_ln_g, sgu_ln_b, sgu_w_s, sgu_b_s, sgu_w_out, ffn_w_gate, ffn_w_up, ffn_w_down, moe_router, moe_w_gate, moe_w_up, moe_w_down):
    bp, lp, d = x_prompt.shape
    bs, ls, _ = x_sample.shape
    depth = ada_w.shape[0]
    n_exp = moe_router.shape[-1]
    tp, ts = bp * lp, bs * ls
    t_all = tp + ts
    assert 2 * n_exp <= LANES and state_pool.shape[2] == POOL_STATE and ls <= SGU_BLOCK

    tm_g = _pick(1024, TOP_K * t_all)
    n_blocks = -(-TOP_K * t_all // tm_g) + n_exp
    nbl = -(-n_blocks // LANES) * LANES

    mods = _ada(jnp.concatenate([c_prompt, c_sample], axis=0), ada_w, ada_b)
    mods = mods.reshape(depth, bp + bs, N_MOD, d)
    mod_p = [mods[l, :bp] for l in range(depth)]
    mod_s = [mods[l, bp:] for l in range(depth)]
    hist = jnp.pad(state_pool, ((0, 0), (0, 0), (HIST_ROWS - POOL_STATE, 0), (0, 0)))

    h_all = jnp.zeros((t_all, d), BF16)
    route_bufs = (jnp.zeros((t_all, d // 2), jnp.uint32), jnp.zeros((t_all, LANES), jnp.int32),
                  jnp.zeros((t_all, LANES), F32))

    ffn_w = [w.astype(BF16) for w in (ffn_w_gate, ffn_w_up, ffn_w_down)]
    moe_w = [w.astype(BF16) for w in (moe_w_gate, moe_w_up, moe_w_down)]

    xp, xs = x_prompt, x_sample
    y_all = None
    pool_p, pool_s, sgu_s = [], [], []
    for l in range(depth):
        jl = l // 2
        g = norm_g[l]
        prev_p = prev_s = None
        if l > 0:
            prev_p = (y_all, mod_p[l - 1], norm_g[l - 1])
            prev_s = (y_all, mod_s[l - 1], norm_g[l - 1])
        if l % 2 == 0:
            w = (pool_w_in[jl].astype(BF16), pool_w_grp[jl].astype(BF16),
                 pool_scale[jl].reshape(1, d), pool_w_out[jl].astype(BF16))
            xp, h_all, st_p = _pool_layer(xp, prev_p, mod_p[l], g, None, *w, h_all, 0)
            xs, h_all, st_s = _pool_layer(xs, prev_s, mod_s[l], g, hist[jl], *w, h_all, tp)
            pool_p.append(st_p[:, HIST_ROWS - POOL_STATE:])
            pool_s.append(st_s[:, HIST_ROWS - POOL_STATE:])
            y_all = _ffn(h_all, *ffn_w, jl)
        else:
            r_hi = moe_router[jl].astype(BF16)
            r_lo = (moe_router[jl] - r_hi.astype(F32)).astype(BF16)
            router_p = jnp.pad(jnp.concatenate([r_hi, r_lo], axis=1), ((0, 0), (0, LANES - 2 * n_exp)))
            w = (sgu_w_in[jl].astype(BF16), sgu_ln_g[jl].reshape(1, d), sgu_ln_b[jl].reshape(1, d),
                 sgu_w_s[jl], jnp.transpose(sgu_b_s[jl]), sgu_w_out[jl].astype(BF16), router_p, n_exp)
            xp, *route_bufs = _sgu_layer(xp, prev_p, mod_p[l], g, *w, route_bufs, 0, False)
            xs, *route_bufs, v_s = _sgu_layer(xs, prev_s, mod_s[l], g, *w, route_bufs, tp, True)
            hp_all, idx_all, gate_all = route_bufs
            sgu_s.append(v_s)
            pos_all, meta = _route(idx_all, tm_g, n_exp, nbl)
            pos0, pos1 = pos_all[:, 0], pos_all[:, 1]
            n_used = meta[1, :1]
            xsort = _dispatch(meta[2, :n_exp], meta[3, :n_exp], n_used, pos0, pos1, hp_all, tm_g, n_blocks, n_exp)
            ys = _moe(meta[0, :n_blocks], n_used, xsort, *moe_w, jl, tm_g)
            y_all = _combine(pos0, pos1, gate_all, ys)
    xp = _final(xp, y_all, mod_p[depth - 1], norm_g[depth - 1], 0)
    xs = _final(xs, y_all, mod_s[depth - 1], norm_g[depth - 1], tp)
    return (xp, xs, jnp.stack(pool_p, axis=0), jnp.stack(pool_s, axis=0), jnp.stack(sgu_s, axis=0))
```

```python
import functools

import jax
import jax.numpy as jnp
from jax import lax
from jax.experimental import pallas as pl
from jax.experimental.pallas import tpu as pltpu

POOL_WINDOWS = (2, 4, 8, 16)
POOL_STATE = POOL_WINDOWS[-1] - 1
CHUNK = 64
SGU_BLOCK = 128
SGU_HEADS = 8
TOP_K = 2
N_MOD = 6
EPS = 1e-6

LANES = 128
SUBLANES = 8
HIST_ROWS = 16
VMEM_LIMIT_BYTES = 58 * 1024 * 1024

F32 = jnp.float32
BF16 = jnp.bfloat16


def _params(*sem):
    return pltpu.CompilerParams(dimension_semantics=sem, vmem_limit_bytes=VMEM_LIMIT_BYTES)


def _resident(shape):
    nd = len(shape)
    return pl.BlockSpec(shape, lambda *_: (0,) * nd, pipeline_mode=pl.Buffered(1))


def _pick(pref, n):
    b = min(pref, n)
    while n % b:
        b //= 2
    return b


def _rms(x, g):
    return x * lax.rsqrt(jnp.mean(x * x, axis=-1, keepdims=True) + EPS) * g


def _sigmoid(x):
    return 1.0 / (1.0 + jnp.exp(-x))


def _dot(a, b):
    return jnp.dot(a, b, preferred_element_type=F32)


def _ada_kernel(c_ref, w_ref, b_ref, o_ref):
    c = c_ref[...]
    a = (c * _sigmoid(c)).astype(BF16)
    o_ref[0] = _dot(a, w_ref[0].astype(BF16)) + b_ref[0]


def _ada(c_all, ada_w, ada_b):
    depth, d, n = ada_w.shape
    bb = c_all.shape[0]
    tn = _pick(1024, n)
    return pl.pallas_call(
        _ada_kernel,
        grid=(depth, n // tn),
        in_specs=[
            pl.BlockSpec((bb, d), lambda l, j: (0, 0)),
            pl.BlockSpec((1, d, tn), lambda l, j: (l, 0, j)),
            pl.BlockSpec((1, 1, tn), lambda l, j: (l, 0, j)),
        ],
        out_specs=pl.BlockSpec((1, bb, tn), lambda l, j: (l, 0, j)),
        out_shape=jax.ShapeDtypeStruct((depth, bb, n), F32),
        compiler_params=_params("arbitrary", "arbitrary"),
        name="ada",
    )(c_all, ada_w, ada_b.reshape(depth, 1, n))


def _front(x_ref, prev, mod, g_ref):
    x = x_ref[0]
    if prev is not None:
        y_ref, modp_ref, gp_ref = prev
        x = x + _rms(y_ref[...], modp_ref[0, 5:6, :] * gp_ref[3:4, :])
    h = _rms(x, g_ref[0:1, :] * (1.0 + mod[1:2, :])) + mod[0:1, :]
    return x, h


def _back(x, y, mod, g_ref):
    x2 = x + _rms(y, mod[2:3, :] * g_ref[1:2, :])
    h2 = _rms(x2, g_ref[2:3, :] * (1.0 + mod[4:5, :])) + mod[3:4, :]
    return x2, h2


def _pool_kernel(*refs, has_prev, has_hist, tm, n_blk, cg):
    it = iter(refs)
    x_ref = next(it)
    prev = (next(it), next(it), next(it)) if has_prev else None
    mod_ref, g_ref = next(it), next(it)
    hist_ref = next(it) if has_hist else None
    win_ref, wgrp_ref, scale_ref, wout_ref = next(it), next(it), next(it), next(it)
    x2_ref, h2_ref, st_ref = next(it), next(it), next(it)
    uext_ref = next(it)

    i = pl.program_id(1)
    mod = mod_ref[0]
    x, h = _front(x_ref, prev, mod, g_ref)
    u = _dot(h.astype(BF16), win_ref[...])

    @pl.when(i == 0)
    def _():
        if has_hist:
            uext_ref[0:HIST_ROWS, :] = hist_ref[0]
        else:
            uext_ref[0:HIST_ROWS, :] = jnp.zeros((HIST_ROWS, u.shape[1]), F32)

    @pl.when(i > 0)
    def _():
        uext_ref[0:HIST_ROWS, :] = uext_ref[tm:tm + HIST_ROWS, :]

    uext_ref[HIST_ROWS:, :] = u

    pos = i * tm + lax.broadcasted_iota(jnp.int32, (tm, 1), 0)
    if has_hist:
        pos = pos + POOL_STATE
    parts = []
    for gi, w in enumerate(POOL_WINDOWS):
        sl = slice(gi * cg, (gi + 1) * cg)
        s = uext_ref[HIST_ROWS:HIST_ROWS + tm, sl]
        for k in range(1, w):
            s = s + uext_ref[HIST_ROWS - k:HIST_ROWS - k + tm, sl]
        cnt = jnp.minimum(pos + 1, w).astype(F32)
        pooled = s / cnt - uext_ref[HIST_ROWS:HIST_ROWS + tm, sl]
        mixed = _dot(pooled.astype(BF16), wgrp_ref[gi]) * scale_ref[:, sl]
        parts.append(mixed.astype(BF16))
    y = _dot(jnp.concatenate(parts, axis=1), wout_ref[...])

    x2, h2 = _back(x, y, mod, g_ref)
    x2_ref[0] = x2
    h2_ref[...] = h2.astype(BF16)

    @pl.when(i == n_blk - 1)
    def _():
        st_ref[0] = uext_ref[tm:tm + HIST_ROWS, :]


def _pool_layer(x, prev, mod, g, hist, w_in, w_grp, scale, w_out, h2_all, row0):
    b, l, d = x.shape
    has_prev, has_hist = prev is not None, hist is not None
    tm = _pick(256, l)
    n_blk = l // tm
    blk0 = row0 // tm
    cg = d // len(POOL_WINDOWS)

    row_map = lambda bi, i: (blk0 + bi * n_blk + i, 0)
    in_specs = [pl.BlockSpec((1, tm, d), lambda bi, i: (bi, i, 0))]
    args = [x]
    if has_prev:
        y_all, mod_prev, g_prev = prev
        in_specs += [pl.BlockSpec((tm, d), row_map),
                     pl.BlockSpec((1, N_MOD, d), lambda bi, i: (bi, 0, 0)),
                     _resident(g_prev.shape)]
        args += [y_all, mod_prev, g_prev]
    in_specs += [pl.BlockSpec((1, N_MOD, d), lambda bi, i: (bi, 0, 0)), _resident(g.shape)]
    args += [mod, g]
    if has_hist:
        in_specs.append(pl.BlockSpec((1, HIST_ROWS, d), lambda bi, i: (bi, 0, 0)))
        args.append(hist)
    in_specs += [_resident(w_in.shape), _resident(w_grp.shape), _resident(scale.shape), _resident(w_out.shape)]
    args += [w_in, w_grp, scale, w_out]

    n_in = len(args)
    in_specs.append(pl.BlockSpec(memory_space=pl.ANY))
    args.append(h2_all)

    def kern(*refs):
        refs = refs[:n_in] + refs[n_in + 1:]
        _pool_kernel(*refs, has_prev=has_prev, has_hist=has_hist, tm=tm, n_blk=n_blk, cg=cg)

    return pl.pallas_call(
        kern,
        grid=(b, n_blk),
        in_specs=in_specs,
        out_specs=[pl.BlockSpec((1, tm, d), lambda bi, i: (bi, i, 0)),
                   pl.BlockSpec((tm, d), row_map),
                   pl.BlockSpec((1, HIST_ROWS, d), lambda bi, i: (bi, 0, 0))],
        out_shape=[jax.ShapeDtypeStruct((b, l, d), F32),
                   jax.ShapeDtypeStruct(h2_all.shape, h2_all.dtype),
                   jax.ShapeDtypeStruct((b, HIST_ROWS, d), F32)],
        scratch_shapes=[pltpu.VMEM((tm + HIST_ROWS, d), F32)],
        input_output_aliases={n_in: 1},
        compiler_params=_params("arbitrary", "arbitrary"),
        name="pool_hist" if has_hist else "pool",
    )(*args)


def _ffn_kernel(x_ref, wg_ref, wu_ref, wd_ref, o_ref):
    j = pl.program_id(1)
    x = x_ref[...]
    gt = _dot(x, wg_ref[...].astype(BF16))
    up = _dot(x, wu_ref[...].astype(BF16))
    a = (gt * _sigmoid(gt) * up).astype(BF16)

    @pl.when(j == 0)
    def _():
        o_ref[...] = jnp.zeros(o_ref.shape, F32)

    o_ref[...] += _dot(a, wd_ref[...].astype(BF16))


def _ffn(h_all, w_gate, w_up, w_down, jl):
    t, d = h_all.shape
    f = w_gate.shape[2]
    tm, tf = _pick(1024, t), _pick(512, f)
    return pl.pallas_call(
        _ffn_kernel,
        grid=(t // tm, f // tf),
        in_specs=[pl.BlockSpec((tm, d), lambda i, j: (i, 0)),
                  pl.BlockSpec((None, d, tf), lambda i, j: (jl, 0, j)),
                  pl.BlockSpec((None, d, tf), lambda i, j: (jl, 0, j)),
                  pl.BlockSpec((None, tf, d), lambda i, j: (jl, j, 0))],
        out_specs=pl.BlockSpec((tm, d), lambda i, j: (i, 0)),
        out_shape=jax.ShapeDtypeStruct((t, d), F32),
        compiler_params=_params("arbitrary", "arbitrary"),
        name="ffn",
    )(h_all, w_gate, w_up, w_down)


def _gelu_tanh(x):
    return 0.5 * x * (1.0 + jnp.tanh(0.7978845608028654 * (x + 0.044715 * (x * x * x))))


def _sgu_kernel(*refs, has_v, tm, n_exp):
    it = iter(refs)
    x_ref = next(it)
    prev = (next(it), next(it), next(it))
    mod_ref, g_ref = next(it), next(it)
    win_ref, lng_ref, lnb_ref, ws_ref, bs_ref, wout_ref, router_ref = (next(it) for _ in range(7))
    x2_ref, hp_ref, idx_ref, gate_ref = next(it), next(it), next(it), next(it)
    v_ref = next(it) if has_v else None

    mod = mod_ref[0]
    x, h = _front(x_ref, prev, mod, g_ref)
    z = _gelu_tanh(_dot(h.astype(BF16), win_ref[...]))
    d = z.shape[1] // 2
    u, v = z[:, :d], z[:, d:]
    vc = v - jnp.mean(v, axis=-1, keepdims=True)
    v = vc * lax.rsqrt(jnp.mean(vc * vc, axis=-1, keepdims=True) + EPS) * lng_ref[...] + lnb_ref[...]
    if has_v:
        v_ref[0] = v
    vb = v.astype(BF16)

    blk = min(tm, SGU_BLOCK)
    ci = lax.broadcasted_iota(jnp.int32, (blk, blk), 0) // CHUNK
    cj = lax.broadcasted_iota(jnp.int32, (blk, blk), 1) // CHUNK
    dh = d // SGU_HEADS
    wm = [jnp.where(cj <= ci, ws_ref[hd, 0:blk, 0:blk], 0.0).astype(BF16) for hd in range(SGU_HEADS)]
    rows = []
    for r in range(tm // blk):
        heads = []
        for hd in range(SGU_HEADS):
            o = _dot(wm[hd], vb[r * blk:(r + 1) * blk, hd * dh:(hd + 1) * dh])
            heads.append(o + bs_ref[0:blk, hd:hd + 1])
        rows.append(jnp.concatenate(heads, axis=1))
    sp = rows[0] if len(rows) == 1 else jnp.concatenate(rows, axis=0)
    y = _dot((u * sp).astype(BF16), wout_ref[...])

    x2, h2 = _back(x, y, mod, g_ref)
    x2_ref[0] = x2

    hb16 = h2.astype(BF16)
    hb = hb16.astype(F32)
    lo = lax.bitcast_convert_type(hb[:, :d // 2], jnp.uint32) >> 16
    hi = lax.bitcast_convert_type(hb[:, d // 2:], jnp.uint32) & jnp.uint32(0xFFFF0000)
    hp_ref[...] = hi | lo

    rest = (h2 - hb).astype(BF16)
    pr = _dot(jnp.concatenate([hb16, rest], axis=0), router_ref[...])
    pr = pr[:tm] + pr[tm:]
    logits = pr + pltpu.roll(pr, LANES - n_exp, 1)
    lane = lax.broadcasted_iota(jnp.int32, logits.shape, 1)
    lg = jnp.where(lane < n_exp, logits, -jnp.inf)
    m1 = jnp.max(lg, axis=1, keepdims=True)
    i1 = jnp.min(jnp.where(lg == m1, lane, LANES), axis=1, keepdims=True)
    lg = jnp.where(lane == i1, -jnp.inf, lg)
    m2 = jnp.max(lg, axis=1, keepdims=True)
    i2 = jnp.min(jnp.where(lg == m2, lane, LANES), axis=1, keepdims=True)
    e = jnp.exp(m2 - m1)
    p1 = 1.0 / (1.0 + e)
    p2 = e / (1.0 + e)
    idx_ref[...] = jnp.where(lane == 0, i1, jnp.where(lane == 1, i2, 0))
    gate_ref[...] = jnp.where(lane == 0, p1, jnp.where(lane == 1, p2, 0.0))


def _sgu_layer(x, prev, mod, g, w_in, ln_g, ln_b, w_s, b_st, w_out, router_p, n_exp, bufs, row0, has_v):
    b, l, d = x.shape
    tm = _pick(256, l)
    n_blk = l // tm
    blk0 = row0 // tm
    y_all, mod_prev, g_prev = prev
    row_map = lambda bi, i: (blk0 + bi * n_blk + i, 0)
    mod_spec = pl.BlockSpec((1, N_MOD, d), lambda bi, i: (bi, 0, 0))

    in_specs = [pl.BlockSpec((1, tm, d), lambda bi, i: (bi, i, 0)),
                pl.BlockSpec((tm, d), row_map), mod_spec, _resident(g_prev.shape),
                mod_spec, _resident(g.shape)]
    args = [x, y_all, mod_prev, g_prev, mod, g]
    for wt in (w_in, ln_g, ln_b, w_s, b_st, w_out, router_p):
        in_specs.append(_resident(wt.shape))
        args.append(wt)

    out_specs = [pl.BlockSpec((1, tm, d), lambda bi, i: (bi, i, 0)),
                 pl.BlockSpec((tm, d // 2), row_map),
                 pl.BlockSpec((tm, LANES), row_map),
                 pl.BlockSpec((tm, LANES), row_map)]
    out_shape = [jax.ShapeDtypeStruct((b, l, d), F32)]
    out_shape += [jax.ShapeDtypeStruct(buf.shape, buf.dtype) for buf in bufs]
    if has_v:
        out_specs.append(pl.BlockSpec((1, tm, d), lambda bi, i: (bi, i, 0)))
        out_shape.append(jax.ShapeDtypeStruct((b, l, d), F32))

    aliases = {}
    n_in = len(args)
    for k, buf in enumerate(bufs):
        in_specs.append(pl.BlockSpec(memory_space=pl.ANY))
        args.append(buf)
        aliases[n_in + k] = 1 + k
    n_alias = len(aliases)

    def kern(*refs):
        refs = refs[:n_in] + refs[n_in + n_alias:]
        _sgu_kernel(*refs, has_v=has_v, tm=tm, n_exp=n_exp)

    return pl.pallas_call(
        kern,
        grid=(b, n_blk),
        in_specs=in_specs,
        out_specs=out_specs,
        out_shape=out_shape,
        input_output_aliases=aliases,
        compiler_params=_params("arbitrary", "arbitrary"),
        name="sgu_v" if has_v else "sgu",
    )(*args)


def _route_kernel(idx_ref, pos_ref, meta_ref, cnt_ref, carry_ref, start_ref, *, tb, tm_g, n_exp, nbl):
    p, i = pl.program_id(0), pl.program_id(1)
    lane = lax.broadcasted_iota(jnp.int32, (tb, LANES), 1)
    idx = idx_ref[...]
    oh1 = lane == idx[:, 0:1]
    oh2 = lane == idx[:, 1:2]
    m = jnp.where(oh1 | oh2, 1.0, 0.0)
    colsum = jnp.sum(m, axis=0, keepdims=True)

    @pl.when((p == 0) & (i == 0))
    def _():
        cnt_ref[...] = jnp.zeros((1, LANES), F32)

    @pl.when(p == 0)
    def _():
        cnt_ref[...] += colsum

    @pl.when((p == 1) & (i == 0))
    def _():
        cnt = cnt_ref[...]
        padded = jnp.ceil(cnt * (1.0 / tm_g)) * float(tm_g)
        lane1 = lax.broadcasted_iota(jnp.int32, (1, LANES), 1)
        blk_row = lax.broadcasted_iota(jnp.int32, (1, nbl), 1).astype(F32) * float(tm_g)
        run = jnp.zeros((1, 1), F32)
        starts = jnp.zeros((1, LANES), F32)
        ends = jnp.zeros((1, LANES), F32)
        blk_exp = jnp.zeros((1, nbl), F32)
        blk_rows = jnp.zeros((1, nbl), F32)
        for ex in range(n_exp):
            starts = jnp.where(lane1 == ex, run, starts)
            real_end = run + jnp.sum(jnp.where(lane1 == ex, cnt, 0.0), axis=1, keepdims=True)
            in_group = blk_row >= run
            run = run + jnp.sum(jnp.where(lane1 == ex, padded, 0.0), axis=1, keepdims=True)
            in_group = in_group & (blk_row < run)
            ends = jnp.where(lane1 == ex, run, ends)
            blk_exp = blk_exp + jnp.where(blk_row >= run, 1.0, 0.0)
            blk_rows = jnp.where(in_group, jnp.minimum(real_end - blk_row, float(tm_g)), blk_rows)
        start_ref[...] = starts
        carry_ref[...] = jnp.zeros((1, LANES), F32)
        meta_ref[...] = jnp.zeros(meta_ref.shape, jnp.int32)
        meta_ref[0:1, :] = jnp.minimum(blk_exp, float(n_exp - 1)).astype(jnp.int32)
        meta_ref[1:2, :] = jnp.broadcast_to(run * (1.0 / tm_g), (1, nbl)).astype(jnp.int32)
        meta_ref[2:3, 0:LANES] = (starts + cnt).astype(jnp.int32)
        meta_ref[3:4, 0:LANES] = ends.astype(jnp.int32)
        meta_ref[4:5, :] = blk_rows.astype(jnp.int32)

    @pl.when(p == 1)
    def _():
        r = lax.broadcasted_iota(jnp.int32, (tb, tb), 0)
        c = lax.broadcasted_iota(jnp.int32, (tb, tb), 1)
        tri = jnp.where(c < r, 1.0, 0.0).astype(BF16)
        rank = _dot(tri, m.astype(BF16)) + carry_ref[...]
        dest = start_ref[...] + rank
        p1 = jnp.sum(jnp.where(oh1, dest, 0.0), axis=1, keepdims=True)
        p2 = jnp.sum(jnp.where(oh2, dest, 0.0), axis=1, keepdims=True)
        pos_ref[...] = jnp.where(lane == 0, p1, jnp.where(lane == 1, p2, 0.0)).astype(jnp.int32)
        carry_ref[...] += colsum


def _route(idx_all, tm_g, n_exp, nbl):
    t = idx_all.shape[0]
    tb = _pick(512, t)
    return pl.pallas_call(
        functools.partial(_route_kernel, tb=tb, tm_g=tm_g, n_exp=n_exp, nbl=nbl),
        grid=(2, t // tb),
        in_specs=[pl.BlockSpec((tb, LANES), lambda p, i: (i, 0))],
        out_specs=[pl.BlockSpec((tb, LANES), lambda p, i: (p * i, 0)),
                   pl.BlockSpec((SUBLANES, nbl), lambda p, i: (0, 0))],
        out_shape=[jax.ShapeDtypeStruct((t, LANES), jnp.int32),
                   jax.ShapeDtypeStruct((SUBLANES, nbl), jnp.int32)],
        scratch_shapes=[pltpu.VMEM((1, LANES), F32)] * 3,
        compiler_params=_params("arbitrary", "arbitrary"),
        name="route",
    )(idx_all)


def _dispatch_kernel(lo_ref, hi_ref, nu_ref, pos0_ref, pos1_ref, h_ref, xs_ref, sem, *, tmd, n_exp, tm_g, n_blocks):
    i = pl.program_id(0)

    def row_copy(src_row, dst_row):
        return pltpu.make_async_copy(h_ref.at[pl.ds(src_row, 1)], xs_ref.at[pl.ds(dst_row, 1)], sem)

    def rows_wait(n):
        pltpu.make_async_copy(h_ref.at[pl.ds(0, n)], xs_ref.at[pl.ds(0, n)], sem).wait()

    @pl.when(i == 0)
    def _():
        def fill_block(b, c):
            cp = pltpu.make_async_copy(h_ref.at[pl.ds(0, tm_g)], xs_ref.at[pl.ds(b * tm_g, tm_g)], sem)
            cp.start()
            cp.wait()
            return c

        lax.fori_loop(nu_ref[0], n_blocks, fill_block, 0)

        for ex in range(n_exp):
            lo, hi = lo_ref[ex], hi_ref[ex]

            def fill(r, c):
                row_copy(0, r).start()
                return c

            def drain(r, c):
                row_copy(0, r).wait()
                return c

            lax.fori_loop(lo, hi, fill, 0)
            lax.fori_loop(lo, hi, drain, 0)

    def issue(r, c):
        row_copy(r, pos0_ref[r]).start(priority=0)
        row_copy(r, pos1_ref[r]).start(priority=1)
        return c

    lax.fori_loop(0, tmd, issue, 0)
    for _ in range(TOP_K):
        rows_wait(tmd)


def _dispatch(pad_lo, pad_hi, n_used, pos0, pos1, hp_all, tm_g, n_blocks, n_exp):
    t, dw = hp_all.shape
    n_rows = n_blocks * tm_g
    tmd = _pick(1024, t)
    assert tmd >= tm_g
    return pl.pallas_call(
        functools.partial(_dispatch_kernel, tmd=tmd, n_exp=n_exp, tm_g=tm_g, n_blocks=n_blocks),
        grid_spec=pltpu.PrefetchScalarGridSpec(
            num_scalar_prefetch=3,
            grid=(t // tmd,),
            in_specs=[pl.BlockSpec((tmd,), lambda i, *_: (i,), memory_space=pltpu.SMEM),
                      pl.BlockSpec((tmd,), lambda i, *_: (i,), memory_space=pltpu.SMEM),
                      pl.BlockSpec((tmd, dw), lambda i, *_: (i, 0))],
            out_specs=pl.BlockSpec(memory_space=pl.ANY),
            scratch_shapes=[pltpu.SemaphoreType.DMA(())],
        ),
        out_shape=jax.ShapeDtypeStruct((n_rows, dw), hp_all.dtype),
        compiler_params=pltpu.CompilerParams(dimension_semantics=("arbitrary",), has_side_effects=True),
        name="dispatch",
    )(pad_lo, pad_hi, n_used, pos0, pos1, hp_all)


def _moe_kernel(be_ref, nu_ref, nv_ref, xs_ref, wg_ref, wu_ref, wd_ref, o_ref, xb_ref):
    b, j = pl.program_id(0), pl.program_id(1)
    tm, dw = xs_ref.shape

    @pl.when(j == 0)
    def _():
        o_ref[...] = jnp.zeros(o_ref.shape, F32)

    def expert_rows(rows):
        @pl.when(j == 0)
        def _():
            w = xs_ref[0:rows, :]
            xb_ref[0:rows, :dw] = lax.bitcast_convert_type(w << 16, F32).astype(BF16)
            xb_ref[0:rows, dw:] = lax.bitcast_convert_type(w & jnp.uint32(0xFFFF0000), F32).astype(BF16)

        x = xb_ref[0:rows, :]
        gt = _dot(x, wg_ref[...].astype(BF16))
        up = _dot(x, wu_ref[...].astype(BF16))
        a = (gt * _sigmoid(gt) * up).astype(BF16)
        o_ref[0:rows, :] += _dot(a, wd_ref[...].astype(BF16))

    live = b < nu_ref[0]
    half = nv_ref[b] <= tm // 2

    @pl.when(live & jnp.logical_not(half))
    def _():
        expert_rows(tm)

    @pl.when(live & half)
    def _():
        expert_rows(tm // 2)


def _moe(blk_exp, n_used, blk_rows, xs, w_gate, w_up, w_down, jl, tm_g):
    n_rows, dw = xs.shape
    _, _, d, f = w_gate.shape
    tf = _pick(512, f)
    n_j = f // tf

    def live(b, nu):
        return jnp.minimum(b, nu[0] - 1)

    def jj(b, j, nu):
        return jnp.where(b < nu[0], j, n_j - 1)

    return pl.pallas_call(
        _moe_kernel,
        grid_spec=pltpu.PrefetchScalarGridSpec(
            num_scalar_prefetch=3,
            grid=(n_rows // tm_g, n_j),
            in_specs=[pl.BlockSpec((tm_g, dw), lambda b, j, be, nu, nv: (live(b, nu), 0)),
                      pl.BlockSpec((None, None, d, tf),
                                   lambda b, j, be, nu, nv: (jl, be[live(b, nu)], 0, jj(b, j, nu))),
                      pl.BlockSpec((None, None, d, tf),
                                   lambda b, j, be, nu, nv: (jl, be[live(b, nu)], 0, jj(b, j, nu))),
                      pl.BlockSpec((None, None, tf, d),
                                   lambda b, j, be, nu, nv: (jl, be[live(b, nu)], jj(b, j, nu), 0))],
            out_specs=pl.BlockSpec((tm_g, d), lambda b, j, be, nu, nv: (b, 0)),
            scratch_shapes=[pltpu.VMEM((tm_g, d), BF16)],
        ),
        out_shape=jax.ShapeDtypeStruct((n_rows, d), F32),
        compiler_params=_params("arbitrary", "arbitrary"),
        name="moe",
    )(blk_exp, n_used, blk_rows, xs, w_gate, w_up, w_down)


def _combine_kernel(pos0_ref, pos1_ref, gate_ref, ys_ref, o_ref, buf_ref, sem, *, tmc):
    def row_copy(k, src_row, r):
        return pltpu.make_async_copy(ys_ref.at[pl.ds(src_row, 1)], buf_ref.at[k, pl.ds(r, 1)], sem)

    def issue(r, c):
        row_copy(0, pos0_ref[r], r).start(priority=0)
        row_copy(1, pos1_ref[r], r).start(priority=1)
        return c

    lax.fori_loop(0, tmc, issue, 0)
    for k in range(TOP_K):
        pltpu.make_async_copy(ys_ref.at[pl.ds(0, tmc)], buf_ref.at[k], sem).wait()
    gate = gate_ref[...]
    o_ref[...] = gate[:, 0:1] * buf_ref[0] + gate[:, 1:2] * buf_ref[1]


def _combine(pos0, pos1, gate_all, ys):
    t = gate_all.shape[0]
    d = ys.shape[1]
    tmc = _pick(1024, t)
    return pl.pallas_call(
        functools.partial(_combine_kernel, tmc=tmc),
        grid=(t // tmc,),
        in_specs=[pl.BlockSpec((tmc,), lambda i: (i,), memory_space=pltpu.SMEM),
                  pl.BlockSpec((tmc,), lambda i: (i,), memory_space=pltpu.SMEM),
                  pl.BlockSpec((tmc, LANES), lambda i: (i, 0)),
                  pl.BlockSpec(memory_space=pl.ANY)],
        out_specs=pl.BlockSpec((tmc, d), lambda i: (i, 0)),
        out_shape=jax.ShapeDtypeStruct((t, d), F32),
        scratch_shapes=[pltpu.VMEM((TOP_K, tmc, d), F32), pltpu.SemaphoreType.DMA(())],
        compiler_params=_params("arbitrary"),
        name="combine",
    )(pos0, pos1, gate_all, ys)


def _final_kernel(x_ref, y_ref, mod_ref, g_ref, o_ref):
    o_ref[0] = x_ref[0] + _rms(y_ref[...], mod_ref[0, 5:6, :] * g_ref[3:4, :])


def _final(x, y_all, mod, g, row0):
    b, l, d = x.shape
    tm = _pick(512, l)
    n_blk = l // tm
    blk0 = row0 // tm
    return pl.pallas_call(
        _final_kernel,
        grid=(b, n_blk),
        in_specs=[pl.BlockSpec((1, tm, d), lambda bi, i: (bi, i, 0)),
                  pl.BlockSpec((tm, d), lambda bi, i: (blk0 + bi * n_blk + i, 0)),
                  pl.BlockSpec((1, N_MOD, d), lambda bi, i: (bi, 0, 0)),
                  _resident(g.shape)],
        out_specs=pl.BlockSpec((1, tm, d), lambda bi, i: (bi, i, 0)),
        out_shape=jax.ShapeDtypeStruct((b, l, d), F32),
        compiler_params=_params("arbitrary", "arbitrary"),
        name="final",
    )(x, y_all, mod, g)


def kernel(x_prompt, x_sample, c_prompt, c_sample, state_pool, ada_w, ada_b, norm_g, pool_w_in, pool_w_grp, pool_scale, pool_w_out, sgu_w_in, sgu_ln_g, sgu_ln_b, sgu_w_s, sgu_b_s, sgu_w_out, ffn_w_gate, ffn_w_up, ffn_w_down, moe_router, moe_w_gate, moe_w_up, moe_w_down):
    bp, lp, d = x_prompt.shape
    bs, ls, _ = x_sample.shape
    depth = ada_w.shape[0]
    n_exp = moe_router.shape[-1]
    tp, ts = bp * lp, bs * ls
    t_all = tp + ts
    assert 2 * n_exp <= LANES and state_pool.shape[2] == POOL_STATE and ls <= SGU_BLOCK

    tm_g = _pick(1024, TOP_K * t_all)
    n_blocks = -(-TOP_K * t_all // tm_g) + n_exp
    nbl = -(-n_blocks // LANES) * LANES

    mods = _ada(jnp.concatenate([c_prompt, c_sample], axis=0), ada_w, ada_b)
    mods = mods.reshape(depth, bp + bs, N_MOD, d)
    mod_p = [mods[l, :bp] for l in range(depth)]
    mod_s = [mods[l, bp:] for l in range(depth)]
    hist = jnp.pad(state_pool, ((0, 0), (0, 0), (HIST_ROWS - POOL_STATE, 0), (0, 0)))

    h_all = jnp.zeros((t_all, d), BF16)
    route_bufs = (jnp.zeros((t_all, d // 2), jnp.uint32), jnp.zeros((t_all, LANES), jnp.int32),
                  jnp.zeros((t_all, LANES), F32))

    ffn_w = (ffn_w_gate, ffn_w_up, ffn_w_down)
    moe_w = (moe_w_gate, moe_w_up, moe_w_down)

    xp, xs = x_prompt, x_sample
    y_all = None
    pool_p, pool_s, sgu_s = [], [], []
    for l in range(depth):
        jl = l // 2
        g = norm_g[l]
        prev_p = prev_s = None
        if l > 0:
            prev_p = (y_all, mod_p[l - 1], norm_g[l - 1])
            prev_s = (y_all, mod_s[l - 1], norm_g[l - 1])
        if l % 2 == 0:
            w = (pool_w_in[jl].astype(BF16), pool_w_grp[jl].astype(BF16),
                 pool_scale[jl].reshape(1, d), pool_w_out[jl].astype(BF16))
            xp, h_all, st_p = _pool_layer(xp, prev_p, mod_p[l], g, None, *w, h_all, 0)
            xs, h_all, st_s = _pool_layer(xs, prev_s, mod_s[l], g, hist[jl], *w, h_all, tp)
            pool_p.append(st_p[:, HIST_ROWS - POOL_STATE:])
            pool_s.append(st_s[:, HIST_ROWS - POOL_STATE:])
            y_all = _ffn(h_all, *ffn_w, jl)
        else:
            r_hi = moe_router[jl].astype(BF16)
            r_lo = (moe_router[jl] - r_hi.astype(F32)).astype(BF16)
            router_p = jnp.pad(jnp.concatenate([r_hi, r_lo], axis=1), ((0, 0), (0, LANES - 2 * n_exp)))
            w = (sgu_w_in[jl].astype(BF16), sgu_ln_g[jl].reshape(1, d), sgu_ln_b[jl].reshape(1, d),
                 sgu_w_s[jl], jnp.transpose(sgu_b_s[jl]), sgu_w_out[jl].astype(BF16), router_p, n_exp)
            xp, *route_bufs = _sgu_layer(xp, prev_p, mod_p[l], g, *w, route_bufs, 0, False)
            xs, *route_bufs, v_s = _sgu_layer(xs, prev_s, mod_s[l], g, *w, route_bufs, tp, True)
            hp_all, idx_all, gate_all = route_bufs
            sgu_s.append(v_s)
            pos_all, meta = _route(idx_all, tm_g, n_exp, nbl)
            pos0, pos1 = pos_all[:, 0], pos_all[:, 1]
            n_used = meta[1, :1]
            xsort = _dispatch(meta[2, :n_exp], meta[3, :n_exp], n_used, pos0, pos1, hp_all, tm_g, n_blocks, n_exp)
            ys = _moe(meta[0, :n_blocks], n_used, meta[4, :n_blocks], xsort, *moe_w, jl, tm_g)
            y_all = _combine(pos0, pos1, gate_all, ys)
    xp = _final(xp, y_all, mod_p[depth - 1], norm_g[depth - 1], 0)
    xs = _final(xs, y_all, mod_s[depth - 1], norm_g[depth - 1], tp)
    return (xp, xs, jnp.stack(pool_p, axis=0), jnp.stack(pool_s, axis=0), jnp.stack(sgu_s, axis=0))
```

```python
import functools

import jax
import jax.numpy as jnp
from jax import lax
from jax.experimental import pallas as pl
from jax.experimental.pallas import tpu as pltpu

POOL_WINDOWS = (2, 4, 8, 16)
POOL_STATE = POOL_WINDOWS[-1] - 1
CHUNK = 64
SGU_BLOCK = 128
SGU_HEADS = 8
TOP_K = 2
N_MOD = 6
EPS = 1e-6

LANES = 128
SUBLANES = 8
HIST_ROWS = 16
VMEM_LIMIT_BYTES = 58 * 1024 * 1024

F32 = jnp.float32
BF16 = jnp.bfloat16


def _params(*sem):
    return pltpu.CompilerParams(dimension_semantics=sem, vmem_limit_bytes=VMEM_LIMIT_BYTES)


def _resident(shape):
    nd = len(shape)
    return pl.BlockSpec(shape, lambda *_: (0,) * nd, pipeline_mode=pl.Buffered(1))


def _pick(pref, n):
    b = min(pref, n)
    while n % b:
        b //= 2
    return b


def _rms(x, g):
    return x * lax.rsqrt(jnp.mean(x * x, axis=-1, keepdims=True) + EPS) * g


def _sigmoid(x):
    return 1.0 / (1.0 + jnp.exp(-x))


def _dot(a, b):
    return jnp.dot(a, b, preferred_element_type=F32)


def _pack(xb):
    dw = xb.shape[1] // 2
    lo = lax.bitcast_convert_type(xb[:, :dw], jnp.uint32) >> 16
    hi = lax.bitcast_convert_type(xb[:, dw:], jnp.uint32) & jnp.uint32(0xFFFF0000)
    return hi | lo


def _unpack(w):
    return (lax.bitcast_convert_type(w << 16, F32), lax.bitcast_convert_type(w & jnp.uint32(0xFFFF0000), F32))


def _ada_kernel(c_ref, w_ref, b_ref, o_ref):
    c = c_ref[...]
    a = (c * _sigmoid(c)).astype(BF16)
    o_ref[0] = _dot(a, w_ref[0].astype(BF16)) + b_ref[0]


def _ada(c_all, ada_w, ada_b):
    depth, d, n = ada_w.shape
    bb = c_all.shape[0]
    tn = _pick(1024, n)
    return pl.pallas_call(
        _ada_kernel,
        grid=(depth, n // tn),
        in_specs=[
            pl.BlockSpec((bb, d), lambda l, j: (0, 0)),
            pl.BlockSpec((1, d, tn), lambda l, j: (l, 0, j)),
            pl.BlockSpec((1, 1, tn), lambda l, j: (l, 0, j)),
        ],
        out_specs=pl.BlockSpec((1, bb, tn), lambda l, j: (l, 0, j)),
        out_shape=jax.ShapeDtypeStruct((depth, bb, n), F32),
        compiler_params=_params("arbitrary", "arbitrary"),
        name="ada",
    )(c_all, ada_w, ada_b.reshape(depth, 1, n))


def _front(x_ref, prev, mod, g_ref):
    x = x_ref[0]
    if prev is not None:
        y_ref, modp_ref, gp_ref = prev
        x = x + _rms(y_ref[...], modp_ref[0, 5:6, :] * gp_ref[3:4, :])
    h = _rms(x, g_ref[0:1, :] * (1.0 + mod[1:2, :])) + mod[0:1, :]
    return x, h


def _back(x, y, mod, g_ref):
    x2 = x + _rms(y, mod[2:3, :] * g_ref[1:2, :])
    h2 = _rms(x2, g_ref[2:3, :] * (1.0 + mod[4:5, :])) + mod[3:4, :]
    return x2, h2


def _pool_kernel(*refs, has_prev, has_hist, tm, n_blk, cg):
    it = iter(refs)
    x_ref = next(it)
    prev = (next(it), next(it), next(it)) if has_prev else None
    mod_ref, g_ref = next(it), next(it)
    hist_ref = next(it) if has_hist else None
    win_ref, wgrp_ref, scale_ref, wout_ref = next(it), next(it), next(it), next(it)
    x2_ref, h2_ref, st_ref = next(it), next(it), next(it)
    uext_ref = next(it)

    i = pl.program_id(1)
    mod = mod_ref[0]
    x, h = _front(x_ref, prev, mod, g_ref)
    u = _dot(h.astype(BF16), win_ref[...])

    @pl.when(i == 0)
    def _():
        if has_hist:
            uext_ref[0:HIST_ROWS, :] = hist_ref[0]
        else:
            uext_ref[0:HIST_ROWS, :] = jnp.zeros((HIST_ROWS, u.shape[1]), F32)

    @pl.when(i > 0)
    def _():
        uext_ref[0:HIST_ROWS, :] = uext_ref[tm:tm + HIST_ROWS, :]

    uext_ref[HIST_ROWS:, :] = u

    pos = i * tm + lax.broadcasted_iota(jnp.int32, (tm, 1), 0)
    if has_hist:
        pos = pos + POOL_STATE
    parts = []
    for gi, w in enumerate(POOL_WINDOWS):
        sl = slice(gi * cg, (gi + 1) * cg)
        s = uext_ref[HIST_ROWS:HIST_ROWS + tm, sl]
        for k in range(1, w):
            s = s + uext_ref[HIST_ROWS - k:HIST_ROWS - k + tm, sl]
        cnt = jnp.minimum(pos + 1, w).astype(F32)
        pooled = s / cnt - uext_ref[HIST_ROWS:HIST_ROWS + tm, sl]
        mixed = _dot(pooled.astype(BF16), wgrp_ref[gi]) * scale_ref[:, sl]
        parts.append(mixed.astype(BF16))
    y = _dot(jnp.concatenate(parts, axis=1), wout_ref[...])

    x2, h2 = _back(x, y, mod, g_ref)
    x2_ref[0] = x2
    h2_ref[...] = h2.astype(BF16)

    @pl.when(i == n_blk - 1)
    def _():
        st_ref[0] = uext_ref[tm:tm + HIST_ROWS, :]


def _pool_layer(x, prev, mod, g, hist, w_in, w_grp, scale, w_out, h2_all, row0):
    b, l, d = x.shape
    has_prev, has_hist = prev is not None, hist is not None
    tm = _pick(256, l)
    n_blk = l // tm
    blk0 = row0 // tm
    cg = d // len(POOL_WINDOWS)

    row_map = lambda bi, i: (blk0 + bi * n_blk + i, 0)
    in_specs = [pl.BlockSpec((1, tm, d), lambda bi, i: (bi, i, 0))]
    args = [x]
    if has_prev:
        y_all, mod_prev, g_prev = prev
        in_specs += [pl.BlockSpec((tm, d), row_map),
                     pl.BlockSpec((1, N_MOD, d), lambda bi, i: (bi, 0, 0)),
                     _resident(g_prev.shape)]
        args += [y_all, mod_prev, g_prev]
    in_specs += [pl.BlockSpec((1, N_MOD, d), lambda bi, i: (bi, 0, 0)), _resident(g.shape)]
    args += [mod, g]
    if has_hist:
        in_specs.append(pl.BlockSpec((1, HIST_ROWS, d), lambda bi, i: (bi, 0, 0)))
        args.append(hist)
    in_specs += [_resident(w_in.shape), _resident(w_grp.shape), _resident(scale.shape), _resident(w_out.shape)]
    args += [w_in, w_grp, scale, w_out]

    n_in = len(args)
    in_specs.append(pl.BlockSpec(memory_space=pl.ANY))
    args.append(h2_all)

    def kern(*refs):
        refs = refs[:n_in] + refs[n_in + 1:]
        _pool_kernel(*refs, has_prev=has_prev, has_hist=has_hist, tm=tm, n_blk=n_blk, cg=cg)

    return pl.pallas_call(
        kern,
        grid=(b, n_blk),
        in_specs=in_specs,
        out_specs=[pl.BlockSpec((1, tm, d), lambda bi, i: (bi, i, 0)),
                   pl.BlockSpec((tm, d), row_map),
                   pl.BlockSpec((1, HIST_ROWS, d), lambda bi, i: (bi, 0, 0))],
        out_shape=[jax.ShapeDtypeStruct((b, l, d), F32),
                   jax.ShapeDtypeStruct(h2_all.shape, h2_all.dtype),
                   jax.ShapeDtypeStruct((b, HIST_ROWS, d), F32)],
        scratch_shapes=[pltpu.VMEM((tm + HIST_ROWS, d), F32)],
        input_output_aliases={n_in: 1},
        compiler_params=_params("arbitrary", "arbitrary"),
        name="pool_hist" if has_hist else "pool",
    )(*args)


def _ffn_kernel(x_ref, wg_ref, wu_ref, wd_ref, o_ref):
    j = pl.program_id(1)
    x = x_ref[...]
    gt = _dot(x, wg_ref[...].astype(BF16))
    up = _dot(x, wu_ref[...].astype(BF16))
    a = (gt * _sigmoid(gt) * up).astype(BF16)

    @pl.when(j == 0)
    def _():
        o_ref[...] = jnp.zeros(o_ref.shape, F32)

    o_ref[...] += _dot(a, wd_ref[...].astype(BF16))


def _ffn(h_all, w_gate, w_up, w_down, jl):
    t, d = h_all.shape
    f = w_gate.shape[2]
    tm, tf = _pick(1024, t), _pick(512, f)
    return pl.pallas_call(
        _ffn_kernel,
        grid=(t // tm, f // tf),
        in_specs=[pl.BlockSpec((tm, d), lambda i, j: (i, 0)),
                  pl.BlockSpec((None, d, tf), lambda i, j: (jl, 0, j)),
                  pl.BlockSpec((None, d, tf), lambda i, j: (jl, 0, j)),
                  pl.BlockSpec((None, tf, d), lambda i, j: (jl, j, 0))],
        out_specs=pl.BlockSpec((tm, d), lambda i, j: (i, 0)),
        out_shape=jax.ShapeDtypeStruct((t, d), F32),
        compiler_params=_params("arbitrary", "arbitrary"),
        name="ffn",
    )(h_all, w_gate, w_up, w_down)


def _gelu_tanh(x):
    return 0.5 * x * (1.0 + jnp.tanh(0.7978845608028654 * (x + 0.044715 * (x * x * x))))


def _sgu_kernel(*refs, has_v, tm, n_exp):
    it = iter(refs)
    x_ref = next(it)
    prev = (next(it), next(it), next(it))
    mod_ref, g_ref = next(it), next(it)
    win_ref, lng_ref, lnb_ref, ws_ref, bs_ref, wout_ref, router_ref = (next(it) for _ in range(7))
    x2_ref, hp_ref, idx_ref, gate_ref = next(it), next(it), next(it), next(it)
    v_ref = next(it) if has_v else None

    mod = mod_ref[0]
    x, h = _front(x_ref, prev, mod, g_ref)
    z = _gelu_tanh(_dot(h.astype(BF16), win_ref[...]))
    d = z.shape[1] // 2
    u, v = z[:, :d], z[:, d:]
    vc = v - jnp.mean(v, axis=-1, keepdims=True)
    v = vc * lax.rsqrt(jnp.mean(vc * vc, axis=-1, keepdims=True) + EPS) * lng_ref[...] + lnb_ref[...]
    if has_v:
        v_ref[0] = v
    vb = v.astype(BF16)

    blk = min(tm, SGU_BLOCK)
    ci = lax.broadcasted_iota(jnp.int32, (blk, blk), 0) // CHUNK
    cj = lax.broadcasted_iota(jnp.int32, (blk, blk), 1) // CHUNK
    dh = d // SGU_HEADS
    wm = [jnp.where(cj <= ci, ws_ref[hd, 0:blk, 0:blk], 0.0).astype(BF16) for hd in range(SGU_HEADS)]
    rows = []
    for r in range(tm // blk):
        heads = []
        for hd in range(SGU_HEADS):
            o = _dot(wm[hd], vb[r * blk:(r + 1) * blk, hd * dh:(hd + 1) * dh])
            heads.append(o + bs_ref[0:blk, hd:hd + 1])
        rows.append(jnp.concatenate(heads, axis=1))
    sp = rows[0] if len(rows) == 1 else jnp.concatenate(rows, axis=0)
    y = _dot((u * sp).astype(BF16), wout_ref[...])

    x2, h2 = _back(x, y, mod, g_ref)
    x2_ref[0] = x2

    hb16 = h2.astype(BF16)
    hb = hb16.astype(F32)
    hp_ref[...] = _pack(hb)

    rest = (h2 - hb).astype(BF16)
    pr = _dot(jnp.concatenate([hb16, rest], axis=0), router_ref[...])
    pr = pr[:tm] + pr[tm:]
    logits = pr + pltpu.roll(pr, LANES - n_exp, 1)
    lane = lax.broadcasted_iota(jnp.int32, logits.shape, 1)
    lg = jnp.where(lane < n_exp, logits, -jnp.inf)
    m1 = jnp.max(lg, axis=1, keepdims=True)
    i1 = jnp.min(jnp.where(lg == m1, lane, LANES), axis=1, keepdims=True)
    lg = jnp.where(lane == i1, -jnp.inf, lg)
    m2 = jnp.max(lg, axis=1, keepdims=True)
    i2 = jnp.min(jnp.where(lg == m2, lane, LANES), axis=1, keepdims=True)
    e = jnp.exp(m2 - m1)
    p1 = 1.0 / (1.0 + e)
    p2 = e / (1.0 + e)
    idx_ref[...] = jnp.where(lane == 0, i1, jnp.where(lane == 1, i2, 0))
    gate_ref[...] = jnp.where(lane == 0, p1, jnp.where(lane == 1, p2, 0.0))


def _sgu_layer(x, prev, mod, g, w_in, ln_g, ln_b, w_s, b_st, w_out, router_p, n_exp, bufs, row0, has_v):
    b, l, d = x.shape
    tm = _pick(256, l)
    n_blk = l // tm
    blk0 = row0 // tm
    y_all, mod_prev, g_prev = prev
    row_map = lambda bi, i: (blk0 + bi * n_blk + i, 0)
    mod_spec = pl.BlockSpec((1, N_MOD, d), lambda bi, i: (bi, 0, 0))

    in_specs = [pl.BlockSpec((1, tm, d), lambda bi, i: (bi, i, 0)),
                pl.BlockSpec((tm, d), row_map), mod_spec, _resident(g_prev.shape),
                mod_spec, _resident(g.shape)]
    args = [x, y_all, mod_prev, g_prev, mod, g]
    for wt in (w_in, ln_g, ln_b, w_s, b_st, w_out, router_p):
        in_specs.append(_resident(wt.shape))
        args.append(wt)

    out_specs = [pl.BlockSpec((1, tm, d), lambda bi, i: (bi, i, 0)),
                 pl.BlockSpec((tm, d // 2), row_map),
                 pl.BlockSpec((tm, LANES), row_map),
                 pl.BlockSpec((tm, LANES), row_map)]
    out_shape = [jax.ShapeDtypeStruct((b, l, d), F32)]
    out_shape += [jax.ShapeDtypeStruct(buf.shape, buf.dtype) for buf in bufs]
    if has_v:
        out_specs.append(pl.BlockSpec((1, tm, d), lambda bi, i: (bi, i, 0)))
        out_shape.append(jax.ShapeDtypeStruct((b, l, d), F32))

    aliases = {}
    n_in = len(args)
    for k, buf in enumerate(bufs):
        in_specs.append(pl.BlockSpec(memory_space=pl.ANY))
        args.append(buf)
        aliases[n_in + k] = 1 + k
    n_alias = len(aliases)

    def kern(*refs):
        refs = refs[:n_in] + refs[n_in + n_alias:]
        _sgu_kernel(*refs, has_v=has_v, tm=tm, n_exp=n_exp)

    return pl.pallas_call(
        kern,
        grid=(b, n_blk),
        in_specs=in_specs,
        out_specs=out_specs,
        out_shape=out_shape,
        input_output_aliases=aliases,
        compiler_params=_params("arbitrary", "arbitrary"),
        name="sgu_v" if has_v else "sgu",
    )(*args)


def _route_kernel(idx_ref, pos_ref, meta_ref, cnt_ref, carry_ref, start_ref, *, tb, tm_g, n_exp, nbl):
    p, i = pl.program_id(0), pl.program_id(1)
    lane = lax.broadcasted_iota(jnp.int32, (tb, LANES), 1)
    idx = idx_ref[...]
    oh1 = lane == idx[:, 0:1]
    oh2 = lane == idx[:, 1:2]
    m = jnp.where(oh1 | oh2, 1.0, 0.0)
    colsum = jnp.sum(m, axis=0, keepdims=True)

    @pl.when((p == 0) & (i == 0))
    def _():
        cnt_ref[...] = jnp.zeros((1, LANES), F32)

    @pl.when(p == 0)
    def _():
        cnt_ref[...] += colsum

    @pl.when((p == 1) & (i == 0))
    def _():
        cnt = cnt_ref[...]
        padded = jnp.ceil(cnt * (1.0 / tm_g)) * float(tm_g)
        lane1 = lax.broadcasted_iota(jnp.int32, (1, LANES), 1)
        blk_row = lax.broadcasted_iota(jnp.int32, (1, nbl), 1).astype(F32) * float(tm_g)
        run = jnp.zeros((1, 1), F32)
        starts = jnp.zeros((1, LANES), F32)
        ends = jnp.zeros((1, LANES), F32)
        blk_exp = jnp.zeros((1, nbl), F32)
        blk_rows = jnp.zeros((1, nbl), F32)
        for ex in range(n_exp):
            starts = jnp.where(lane1 == ex, run, starts)
            real_end = run + jnp.sum(jnp.where(lane1 == ex, cnt, 0.0), axis=1, keepdims=True)
            in_group = blk_row >= run
            run = run + jnp.sum(jnp.where(lane1 == ex, padded, 0.0), axis=1, keepdims=True)
            in_group = in_group & (blk_row < run)
            ends = jnp.where(lane1 == ex, run, ends)
            blk_exp = blk_exp + jnp.where(blk_row >= run, 1.0, 0.0)
            blk_rows = jnp.where(in_group, jnp.minimum(real_end - blk_row, float(tm_g)), blk_rows)
        start_ref[...] = starts
        carry_ref[...] = jnp.zeros((1, LANES), F32)
        meta_ref[...] = jnp.zeros(meta_ref.shape, jnp.int32)
        meta_ref[0:1, :] = jnp.minimum(blk_exp, float(n_exp - 1)).astype(jnp.int32)
        meta_ref[1:2, :] = jnp.broadcast_to(run * (1.0 / tm_g), (1, nbl)).astype(jnp.int32)
        meta_ref[2:3, 0:LANES] = (starts + cnt).astype(jnp.int32)
        meta_ref[3:4, 0:LANES] = ends.astype(jnp.int32)
        meta_ref[4:5, :] = blk_rows.astype(jnp.int32)

    @pl.when(p == 1)
    def _():
        r = lax.broadcasted_iota(jnp.int32, (tb, tb), 0)
        c = lax.broadcasted_iota(jnp.int32, (tb, tb), 1)
        tri = jnp.where(c < r, 1.0, 0.0).astype(BF16)
        rank = _dot(tri, m.astype(BF16)) + carry_ref[...]
        dest = start_ref[...] + rank
        p1 = jnp.sum(jnp.where(oh1, dest, 0.0), axis=1, keepdims=True)
        p2 = jnp.sum(jnp.where(oh2, dest, 0.0), axis=1, keepdims=True)
        pos_ref[...] = jnp.where(lane == 0, p1, jnp.where(lane == 1, p2, 0.0)).astype(jnp.int32)
        carry_ref[...] += colsum


def _route(idx_all, tm_g, n_exp, nbl):
    t = idx_all.shape[0]
    tb = _pick(1024, t)
    return pl.pallas_call(
        functools.partial(_route_kernel, tb=tb, tm_g=tm_g, n_exp=n_exp, nbl=nbl),
        grid=(2, t // tb),
        in_specs=[pl.BlockSpec((tb, LANES), lambda p, i: (i, 0))],
        out_specs=[pl.BlockSpec((tb, LANES), lambda p, i: (p * i, 0)),
                   pl.BlockSpec((SUBLANES, nbl), lambda p, i: (0, 0))],
        out_shape=[jax.ShapeDtypeStruct((t, LANES), jnp.int32),
                   jax.ShapeDtypeStruct((SUBLANES, nbl), jnp.int32)],
        scratch_shapes=[pltpu.VMEM((1, LANES), F32)] * 3,
        compiler_params=_params("arbitrary", "arbitrary"),
        name="route",
    )(idx_all)


def _dispatch_kernel(lo_ref, hi_ref, nu_ref, pos0_ref, pos1_ref, h_ref, xs_ref, sem, *, tmd, n_exp, tm_g, n_blocks):
    i = pl.program_id(0)

    def row_copy(src_row, dst_row):
        return pltpu.make_async_copy(h_ref.at[pl.ds(src_row, 1)], xs_ref.at[pl.ds(dst_row, 1)], sem)

    def rows_wait(n):
        pltpu.make_async_copy(h_ref.at[pl.ds(0, n)], xs_ref.at[pl.ds(0, n)], sem).wait()

    @pl.when(i == 0)
    def _():
        def fill_block(b, c):
            cp = pltpu.make_async_copy(h_ref.at[pl.ds(0, tm_g)], xs_ref.at[pl.ds(b * tm_g, tm_g)], sem)
            cp.start()
            cp.wait()
            return c

        lax.fori_loop(nu_ref[0], n_blocks, fill_block, 0)

        for ex in range(n_exp):
            lo, hi = lo_ref[ex], hi_ref[ex]

            def fill(r, c):
                row_copy(0, r).start()
                return c

            def drain(r, c):
                row_copy(0, r).wait()
                return c

            lax.fori_loop(lo, hi, fill, 0)
            lax.fori_loop(lo, hi, drain, 0)

    def issue(r, c):
        row_copy(r, pos0_ref[r]).start(priority=0)
        row_copy(r, pos1_ref[r]).start(priority=1)
        return c

    lax.fori_loop(0, tmd, issue, 0)
    for _ in range(TOP_K):
        rows_wait(tmd)


def _dispatch(pad_lo, pad_hi, n_used, pos0, pos1, hp_all, tm_g, n_blocks, n_exp):
    t, dw = hp_all.shape
    n_rows = n_blocks * tm_g
    tmd = _pick(1024, t)
    assert tmd >= tm_g
    return pl.pallas_call(
        functools.partial(_dispatch_kernel, tmd=tmd, n_exp=n_exp, tm_g=tm_g, n_blocks=n_blocks),
        grid_spec=pltpu.PrefetchScalarGridSpec(
            num_scalar_prefetch=3,
            grid=(t // tmd,),
            in_specs=[pl.BlockSpec((tmd,), lambda i, *_: (i,), memory_space=pltpu.SMEM),
                      pl.BlockSpec((tmd,), lambda i, *_: (i,), memory_space=pltpu.SMEM),
                      pl.BlockSpec((tmd, dw), lambda i, *_: (i, 0))],
            out_specs=pl.BlockSpec(memory_space=pl.ANY),
            scratch_shapes=[pltpu.SemaphoreType.DMA(())],
        ),
        out_shape=jax.ShapeDtypeStruct((n_rows, dw), hp_all.dtype),
        compiler_params=pltpu.CompilerParams(dimension_semantics=("arbitrary",), has_side_effects=True),
        name="dispatch",
    )(pad_lo, pad_hi, n_used, pos0, pos1, hp_all)


def _moe_kernel(be_ref, nu_ref, nv_ref, xs_ref, wg_ref, wu_ref, wd_ref, o_ref, xb_ref, acc_ref, *, n_j):
    b, j = pl.program_id(0), pl.program_id(1)
    tm, dw = xs_ref.shape

    @pl.when(j == 0)
    def _():
        acc_ref[...] = jnp.zeros(acc_ref.shape, F32)

    def expert_rows(rows):
        @pl.when(j == 0)
        def _():
            lo, hi = _unpack(xs_ref[0:rows, :])
            xb_ref[0:rows, :dw] = lo.astype(BF16)
            xb_ref[0:rows, dw:] = hi.astype(BF16)

        x = xb_ref[0:rows, :]
        gt = _dot(x, wg_ref[...].astype(BF16))
        up = _dot(x, wu_ref[...].astype(BF16))
        a = (gt * _sigmoid(gt) * up).astype(BF16)
        acc_ref[0:rows, :] += _dot(a, wd_ref[...].astype(BF16))

    live = b < nu_ref[0]
    half = nv_ref[b] <= tm // 2

    @pl.when(live & jnp.logical_not(half))
    def _():
        expert_rows(tm)

    @pl.when(live & half)
    def _():
        expert_rows(tm // 2)

    @pl.when(j == n_j - 1)
    def _():
        o_ref[...] = _pack(acc_ref[...].astype(BF16).astype(F32))


def _moe(blk_exp, n_used, blk_rows, xs, w_gate, w_up, w_down, jl, tm_g):
    n_rows, dw = xs.shape
    _, _, d, f = w_gate.shape
    tf = _pick(512, f)
    n_j = f // tf

    def live(b, nu):
        return jnp.minimum(b, nu[0] - 1)

    def jj(b, j, nu):
        return jnp.where(b < nu[0], j, n_j - 1)

    return pl.pallas_call(
        functools.partial(_moe_kernel, n_j=n_j),
        grid_spec=pltpu.PrefetchScalarGridSpec(
            num_scalar_prefetch=3,
            grid=(n_rows // tm_g, n_j),
            in_specs=[pl.BlockSpec((tm_g, dw), lambda b, j, be, nu, nv: (live(b, nu), 0)),
                      pl.BlockSpec((None, None, d, tf),
                                   lambda b, j, be, nu, nv: (jl, be[live(b, nu)], 0, jj(b, j, nu))),
                      pl.BlockSpec((None, None, d, tf),
                                   lambda b, j, be, nu, nv: (jl, be[live(b, nu)], 0, jj(b, j, nu))),
                      pl.BlockSpec((None, None, tf, d),
                                   lambda b, j, be, nu, nv: (jl, be[live(b, nu)], jj(b, j, nu), 0))],
            out_specs=pl.BlockSpec((tm_g, dw), lambda b, j, be, nu, nv: (b, 0)),
            scratch_shapes=[pltpu.VMEM((tm_g, d), BF16), pltpu.VMEM((tm_g, d), F32)],
        ),
        out_shape=jax.ShapeDtypeStruct((n_rows, dw), jnp.uint32),
        compiler_params=_params("arbitrary", "arbitrary"),
        name="moe",
    )(blk_exp, n_used, blk_rows, xs, w_gate, w_up, w_down)


def _combine_kernel(pos0_ref, pos1_ref, gate_ref, ys_ref, o_ref, buf_ref, sem, *, tmc):
    def row_copy(k, src_row, r):
        return pltpu.make_async_copy(ys_ref.at[pl.ds(src_row, 1)], buf_ref.at[k, pl.ds(r, 1)], sem)

    def issue(r, c):
        row_copy(0, pos0_ref[r], r).start(priority=0)
        row_copy(1, pos1_ref[r], r).start(priority=1)
        return c

    lax.fori_loop(0, tmc, issue, 0)
    for k in range(TOP_K):
        pltpu.make_async_copy(ys_ref.at[pl.ds(0, tmc)], buf_ref.at[k], sem).wait()
    gate = gate_ref[...]
    g0, g1 = gate[:, 0:1], gate[:, 1:2]
    dw = buf_ref.shape[2]
    lo0, hi0 = _unpack(buf_ref[0])
    lo1, hi1 = _unpack(buf_ref[1])
    o_ref[:, :dw] = g0 * lo0 + g1 * lo1
    o_ref[:, dw:] = g0 * hi0 + g1 * hi1


def _combine(pos0, pos1, gate_all, ys):
    t = gate_all.shape[0]
    dw = ys.shape[1]
    d = 2 * dw
    tmc = _pick(1024, t)
    return pl.pallas_call(
        functools.partial(_combine_kernel, tmc=tmc),
        grid=(t // tmc,),
        in_specs=[pl.BlockSpec((tmc,), lambda i: (i,), memory_space=pltpu.SMEM),
                  pl.BlockSpec((tmc,), lambda i: (i,), memory_space=pltpu.SMEM),
                  pl.BlockSpec((tmc, LANES), lambda i: (i, 0)),
                  pl.BlockSpec(memory_space=pl.ANY)],
        out_specs=pl.BlockSpec((tmc, d), lambda i: (i, 0)),
        out_shape=jax.ShapeDtypeStruct((t, d), F32),
        scratch_shapes=[pltpu.VMEM((TOP_K, tmc, dw), jnp.uint32), pltpu.SemaphoreType.DMA(())],
        compiler_params=_params("arbitrary"),
        name="combine",
    )(pos0, pos1, gate_all, ys)


def _final_kernel(x_ref, y_ref, mod_ref, g_ref, o_ref):
    o_ref[0] = x_ref[0] + _rms(y_ref[...], mod_ref[0, 5:6, :] * g_ref[3:4, :])


def _final(x, y_all, mod, g, row0):
    b, l, d = x.shape
    tm = _pick(512, l)
    n_blk = l // tm
    blk0 = row0 // tm
    return pl.pallas_call(
        _final_kernel,
        grid=(b, n_blk),
        in_specs=[pl.BlockSpec((1, tm, d), lambda bi, i: (bi, i, 0)),
                  pl.BlockSpec((tm, d), lambda bi, i: (blk0 + bi * n_blk + i, 0)),
                  pl.BlockSpec((1, N_MOD, d), lambda bi, i: (bi, 0, 0)),
                  _resident(g.shape)],
        out_specs=pl.BlockSpec((1, tm, d), lambda bi, i: (bi, i, 0)),
        out_shape=jax.ShapeDtypeStruct((b, l, d), F32),
        compiler_params=_params("arbitrary", "arbitrary"),
        name="final",
    )(x, y_all, mod, g)


def kernel(x_prompt, x_sample, c_prompt, c_sample, state_pool, ada_w, ada_b, norm_g, pool_w_in, pool_w_grp, pool_scale, pool_w_out, sgu_w_in, sgu_ln_g, sgu_ln_b, sgu_w_s, sgu_b_s, sgu_w_out, ffn_w_gate, ffn_w_up, ffn_w_down, moe_router, moe_w_gate, moe_w_up, moe_w_down):
    bp, lp, d = x_prompt.shape
    bs, ls, _ = x_sample.shape
    depth = ada_w.shape[0]
    n_exp = moe_router.shape[-1]
    tp, ts = bp * lp, bs * ls
    t_all = tp + ts
    assert 2 * n_exp <= LANES and state_pool.shape[2] == POOL_STATE and ls <= SGU_BLOCK

    tm_g = _pick(1024, TOP_K * t_all)
    n_blocks = -(-TOP_K * t_all // tm_g) + n_exp
    nbl = -(-n_blocks // LANES) * LANES

    mods = _ada(jnp.concatenate([c_prompt, c_sample], axis=0), ada_w, ada_b)
    mods = mods.reshape(depth, bp + bs, N_MOD, d)
    mod_p = [mods[l, :bp] for l in range(depth)]
    mod_s = [mods[l, bp:] for l in range(depth)]
    hist = jnp.pad(state_pool, ((0, 0), (0, 0), (HIST_ROWS - POOL_STATE, 0), (0, 0)))

    h_all = jnp.zeros((t_all, d), BF16)
    route_bufs = (jnp.zeros((t_all, d // 2), jnp.uint32), jnp.zeros((t_all, LANES), jnp.int32),
                  jnp.zeros((t_all, LANES), F32))

    ffn_w = (ffn_w_gate, ffn_w_up, ffn_w_down)
    moe_w = (moe_w_gate, moe_w_up, moe_w_down)

    xp, xs = x_prompt, x_sample
    y_all = None
    pool_p, pool_s, sgu_s = [], [], []
    for l in range(depth):
        jl = l // 2
        g = norm_g[l]
        prev_p = prev_s = None
        if l > 0:
            prev_p = (y_all, mod_p[l - 1], norm_g[l - 1])
            prev_s = (y_all, mod_s[l - 1], norm_g[l - 1])
        if l % 2 == 0:
            w = (pool_w_in[jl].astype(BF16), pool_w_grp[jl].astype(BF16),
                 pool_scale[jl].reshape(1, d), pool_w_out[jl].astype(BF16))
            xp, h_all, st_p = _pool_layer(xp, prev_p, mod_p[l], g, None, *w, h_all, 0)
            xs, h_all, st_s = _pool_layer(xs, prev_s, mod_s[l], g, hist[jl], *w, h_all, tp)
            pool_p.append(st_p[:, HIST_ROWS - POOL_STATE:])
            pool_s.append(st_s[:, HIST_ROWS - POOL_STATE:])
            y_all = _ffn(h_all, *ffn_w, jl)
        else:
            r_hi = moe_router[jl].astype(BF16)
            r_lo = (moe_router[jl] - r_hi.astype(F32)).astype(BF16)
            router_p = jnp.pad(jnp.concatenate([r_hi, r_lo], axis=1), ((0, 0), (0, LANES - 2 * n_exp)))
            w = (sgu_w_in[jl].astype(BF16), sgu_ln_g[jl].reshape(1, d), sgu_ln_b[jl].reshape(1, d),
                 sgu_w_s[jl], jnp.transpose(sgu_b_s[jl]), sgu_w_out[jl].astype(BF16), router_p, n_exp)
            xp, *route_bufs = _sgu_layer(xp, prev_p, mod_p[l], g, *w, route_bufs, 0, False)
            xs, *route_bufs, v_s = _sgu_layer(xs, prev_s, mod_s[l], g, *w, route_bufs, tp, True)
            hp_all, idx_all, gate_all = route_bufs
            sgu_s.append(v_s)
            pos_all, meta = _route(idx_all, tm_g, n_exp, nbl)
            pos0, pos1 = pos_all[:, 0], pos_all[:, 1]
            n_used = meta[1, :1]
            xsort = _dispatch(meta[2, :n_exp], meta[3, :n_exp], n_used, pos0, pos1, hp_all, tm_g, n_blocks, n_exp)
            ys = _moe(meta[0, :n_blocks], n_used, meta[4, :n_blocks], xsort, *moe_w, jl, tm_g)
            y_all = _combine(pos0, pos1, gate_all, ys)
    xp = _final(xp, y_all, mod_p[depth - 1], norm_g[depth - 1], 0)
    xs = _final(xs, y_all, mod_s[depth - 1], norm_g[depth - 1], tp)
    return (xp, xs, jnp.stack(pool_p, axis=0), jnp.stack(pool_s, axis=0), jnp.stack(sgu_s, axis=0))
```

```python
import functools

import jax
import jax.numpy as jnp
from jax import lax
from jax.experimental import pallas as pl
from jax.experimental.pallas import tpu as pltpu

POOL_WINDOWS = (2, 4, 8, 16)
POOL_STATE = POOL_WINDOWS[-1] - 1
CHUNK = 64
SGU_BLOCK = 128
SGU_HEADS = 8
TOP_K = 2
N_MOD = 6
EPS = 1e-6

LANES = 128
SUBLANES = 8
HIST_ROWS = 16
VMEM_LIMIT_BYTES = 58 * 1024 * 1024

F32 = jnp.float32
BF16 = jnp.bfloat16


def _params(*sem):
    return pltpu.CompilerParams(dimension_semantics=sem, vmem_limit_bytes=VMEM_LIMIT_BYTES)


def _resident(shape):
    nd = len(shape)
    return pl.BlockSpec(shape, lambda *_: (0,) * nd, pipeline_mode=pl.Buffered(1))


def _pick(pref, n):
    b = min(pref, n)
    while n % b:
        b //= 2
    return b


def _rms(x, g):
    return x * lax.rsqrt(jnp.mean(x * x, axis=-1, keepdims=True) + EPS) * g


def _sigmoid(x):
    return 1.0 / (1.0 + jnp.exp(-x))


def _dot(a, b):
    return jnp.dot(a, b, preferred_element_type=F32)


def _ada_kernel(c_ref, w_ref, b_ref, o_ref):
    c = c_ref[...]
    a = (c * _sigmoid(c)).astype(BF16)
    o_ref[0] = _dot(a, w_ref[0].astype(BF16)) + b_ref[0]


def _ada(c_all, ada_w, ada_b):
    depth, d, n = ada_w.shape
    bb = c_all.shape[0]
    tn = _pick(1024, n)
    return pl.pallas_call(
        _ada_kernel,
        grid=(depth, n // tn),
        in_specs=[
            pl.BlockSpec((bb, d), lambda l, j: (0, 0)),
            pl.BlockSpec((1, d, tn), lambda l, j: (l, 0, j)),
            pl.BlockSpec((1, 1, tn), lambda l, j: (l, 0, j)),
        ],
        out_specs=pl.BlockSpec((1, bb, tn), lambda l, j: (l, 0, j)),
        out_shape=jax.ShapeDtypeStruct((depth, bb, n), F32),
        compiler_params=_params("arbitrary", "arbitrary"),
        name="ada",
    )(c_all, ada_w, ada_b.reshape(depth, 1, n))


def _front(x_ref, prev, mod, g_ref):
    x = x_ref[0]
    if prev is not None:
        y_ref, modp_ref, gp_ref = prev
        x = x + _rms(y_ref[...], modp_ref[0, 5:6, :] * gp_ref[3:4, :])
    h = _rms(x, g_ref[0:1, :] * (1.0 + mod[1:2, :])) + mod[0:1, :]
    return x, h


def _back(x, y, mod, g_ref):
    x2 = x + _rms(y, mod[2:3, :] * g_ref[1:2, :])
    h2 = _rms(x2, g_ref[2:3, :] * (1.0 + mod[4:5, :])) + mod[3:4, :]
    return x2, h2


def _pool_kernel(*refs, has_prev, has_hist, tm, n_blk, cg):
    it = iter(refs)
    x_ref = next(it)
    prev = (next(it), next(it), next(it)) if has_prev else None
    mod_ref, g_ref = next(it), next(it)
    hist_ref = next(it) if has_hist else None
    win_ref, wgrp_ref, scale_ref, wout_ref = next(it), next(it), next(it), next(it)
    x2_ref, h2_ref, st_ref = next(it), next(it), next(it)
    uext_ref = next(it)

    i = pl.program_id(1)
    mod = mod_ref[0]
    x, h = _front(x_ref, prev, mod, g_ref)
    u = _dot(h.astype(BF16), win_ref[...])

    @pl.when(i == 0)
    def _():
        if has_hist:
            uext_ref[0:HIST_ROWS, :] = hist_ref[0]
        else:
            uext_ref[0:HIST_ROWS, :] = jnp.zeros((HIST_ROWS, u.shape[1]), F32)

    @pl.when(i > 0)
    def _():
        uext_ref[0:HIST_ROWS, :] = uext_ref[tm:tm + HIST_ROWS, :]

    uext_ref[HIST_ROWS:, :] = u

    pos = i * tm + lax.broadcasted_iota(jnp.int32, (tm, 1), 0)
    if has_hist:
        pos = pos + POOL_STATE
    parts = []
    for gi, w in enumerate(POOL_WINDOWS):
        sl = slice(gi * cg, (gi + 1) * cg)
        s = uext_ref[:, sl]
        k = 1
        while k < w:
            s = s + pltpu.roll(s, k, 0)
            k *= 2
        cnt = jnp.minimum(pos + 1, w).astype(F32)
        pooled = s[HIST_ROWS:, :] / cnt - uext_ref[HIST_ROWS:HIST_ROWS + tm, sl]
        mixed = _dot(pooled.astype(BF16), wgrp_ref[gi]) * scale_ref[:, sl]
        parts.append(mixed.astype(BF16))
    y = _dot(jnp.concatenate(parts, axis=1), wout_ref[...])

    x2, h2 = _back(x, y, mod, g_ref)
    x2_ref[0] = x2
    h2_ref[...] = h2.astype(BF16)

    @pl.when(i == n_blk - 1)
    def _():
        st_ref[0] = uext_ref[tm:tm + HIST_ROWS, :]


def _pool_layer(x, prev, mod, g, hist, w_in, w_grp, scale, w_out, h2_all, row0):
    b, l, d = x.shape
    has_prev, has_hist = prev is not None, hist is not None
    tm = _pick(256, l)
    n_blk = l // tm
    blk0 = row0 // tm
    cg = d // len(POOL_WINDOWS)

    row_map = lambda bi, i: (blk0 + bi * n_blk + i, 0)
    in_specs = [pl.BlockSpec((1, tm, d), lambda bi, i: (bi, i, 0))]
    args = [x]
    if has_prev:
        y_all, mod_prev, g_prev = prev
        in_specs += [pl.BlockSpec((tm, d), row_map),
                     pl.BlockSpec((1, N_MOD, d), lambda bi, i: (bi, 0, 0)),
                     _resident(g_prev.shape)]
        args += [y_all, mod_prev, g_prev]
    in_specs += [pl.BlockSpec((1, N_MOD, d), lambda bi, i: (bi, 0, 0)), _resident(g.shape)]
    args += [mod, g]
    if has_hist:
        in_specs.append(pl.BlockSpec((1, HIST_ROWS, d), lambda bi, i: (bi, 0, 0)))
        args.append(hist)
    in_specs += [_resident(w_in.shape), _resident(w_grp.shape), _resident(scale.shape), _resident(w_out.shape)]
    args += [w_in, w_grp, scale, w_out]

    n_in = len(args)
    in_specs.append(pl.BlockSpec(memory_space=pl.ANY))
    args.append(h2_all)

    def kern(*refs):
        refs = refs[:n_in] + refs[n_in + 1:]
        _pool_kernel(*refs, has_prev=has_prev, has_hist=has_hist, tm=tm, n_blk=n_blk, cg=cg)

    return pl.pallas_call(
        kern,
        grid=(b, n_blk),
        in_specs=in_specs,
        out_specs=[pl.BlockSpec((1, tm, d), lambda bi, i: (bi, i, 0)),
                   pl.BlockSpec((tm, d), row_map),
                   pl.BlockSpec((1, HIST_ROWS, d), lambda bi, i: (bi, 0, 0))],
        out_shape=[jax.ShapeDtypeStruct((b, l, d), F32),
                   jax.ShapeDtypeStruct(h2_all.shape, h2_all.dtype),
                   jax.ShapeDtypeStruct((b, HIST_ROWS, d), F32)],
        scratch_shapes=[pltpu.VMEM((tm + HIST_ROWS, d), F32)],
        input_output_aliases={n_in: 1},
        compiler_params=_params("arbitrary", "arbitrary"),
        name="pool_hist" if has_hist else "pool",
    )(*args)


def _ffn_kernel(x_ref, wg_ref, wu_ref, wd_ref, o_ref):
    j = pl.program_id(1)
    x = x_ref[...]
    gt = _dot(x, wg_ref[...].astype(BF16))
    up = _dot(x, wu_ref[...].astype(BF16))
    a = (gt * _sigmoid(gt) * up).astype(BF16)

    @pl.when(j == 0)
    def _():
        o_ref[...] = jnp.zeros(o_ref.shape, F32)

    o_ref[...] += _dot(a, wd_ref[...].astype(BF16))


def _ffn(h_all, w_gate, w_up, w_down, jl):
    t, d = h_all.shape
    f = w_gate.shape[2]
    tm, tf = _pick(1024, t), _pick(512, f)
    return pl.pallas_call(
        _ffn_kernel,
        grid=(t // tm, f // tf),
        in_specs=[pl.BlockSpec((tm, d), lambda i, j: (i, 0)),
                  pl.BlockSpec((None, d, tf), lambda i, j: (jl, 0, j)),
                  pl.BlockSpec((None, d, tf), lambda i, j: (jl, 0, j)),
                  pl.BlockSpec((None, tf, d), lambda i, j: (jl, j, 0))],
        out_specs=pl.BlockSpec((tm, d), lambda i, j: (i, 0)),
        out_shape=jax.ShapeDtypeStruct((t, d), F32),
        compiler_params=_params("arbitrary", "arbitrary"),
        name="ffn",
    )(h_all, w_gate, w_up, w_down)


GELU_C = 0.7978845608028654
GELU_A = 0.044715


def _gelu_tanh(x):
    return x * (0.5 + 0.5 * jnp.tanh(x * (GELU_C + (GELU_C * GELU_A) * (x * x))))


def _sgu_kernel(*refs, has_v, tm, n_exp):
    it = iter(refs)
    x_ref = next(it)
    prev = (next(it), next(it), next(it))
    mod_ref, g_ref = next(it), next(it)
    win_ref, lng_ref, lnb_ref, ws_ref, bs_ref, wout_ref, router_ref = (next(it) for _ in range(7))
    x2_ref, hp_ref, idx_ref, gate_ref = next(it), next(it), next(it), next(it)
    v_ref = next(it) if has_v else None

    mod = mod_ref[0]
    x, h = _front(x_ref, prev, mod, g_ref)
    z = _gelu_tanh(_dot(h.astype(BF16), win_ref[...]))
    d = z.shape[1] // 2
    u, v = z[:, :d], z[:, d:]
    vc = v - jnp.mean(v, axis=-1, keepdims=True)
    v = vc * lax.rsqrt(jnp.mean(vc * vc, axis=-1, keepdims=True) + EPS) * lng_ref[...] + lnb_ref[...]
    if has_v:
        v_ref[0] = v
    vb = v.astype(BF16)

    blk = min(tm, SGU_BLOCK)
    ci = lax.broadcasted_iota(jnp.int32, (blk, blk), 0) // CHUNK
    cj = lax.broadcasted_iota(jnp.int32, (blk, blk), 1) // CHUNK
    dh = d // SGU_HEADS
    wm = [jnp.where(cj <= ci, ws_ref[hd, 0:blk, 0:blk], 0.0).astype(BF16) for hd in range(SGU_HEADS)]
    rows = []
    for r in range(tm // blk):
        heads = []
        for hd in range(SGU_HEADS):
            o = _dot(wm[hd], vb[r * blk:(r + 1) * blk, hd * dh:(hd + 1) * dh])
            heads.append(o + bs_ref[0:blk, hd:hd + 1])
        rows.append(jnp.concatenate(heads, axis=1))
    sp = rows[0] if len(rows) == 1 else jnp.concatenate(rows, axis=0)
    y = _dot((u * sp).astype(BF16), wout_ref[...])

    x2, h2 = _back(x, y, mod, g_ref)
    x2_ref[0] = x2

    hb16 = h2.astype(BF16)
    hb = hb16.astype(F32)
    lo = lax.bitcast_convert_type(hb[:, :d // 2], jnp.uint32) >> 16
    hi = lax.bitcast_convert_type(hb[:, d // 2:], jnp.uint32) & jnp.uint32(0xFFFF0000)
    hp_ref[...] = hi | lo

    rest = (h2 - hb).astype(BF16)
    pr = _dot(jnp.concatenate([hb16, rest], axis=0), router_ref[...])
    pr = pr[:tm] + pr[tm:]
    logits = pr + pltpu.roll(pr, LANES - n_exp, 1)
    lane = lax.broadcasted_iota(jnp.int32, logits.shape, 1)
    lg = jnp.where(lane < n_exp, logits, -jnp.inf)
    m1 = jnp.max(lg, axis=1, keepdims=True)
    i1 = jnp.min(jnp.where(lg == m1, lane, LANES), axis=1, keepdims=True)
    lg = jnp.where(lane == i1, -jnp.inf, lg)
    m2 = jnp.max(lg, axis=1, keepdims=True)
    i2 = jnp.min(jnp.where(lg == m2, lane, LANES), axis=1, keepdims=True)
    e = jnp.exp(m2 - m1)
    p1 = 1.0 / (1.0 + e)
    p2 = e / (1.0 + e)
    idx_ref[...] = jnp.where(lane == 0, i1, jnp.where(lane == 1, i2, 0))
    gate_ref[...] = jnp.where(lane == 0, p1, jnp.where(lane == 1, p2, 0.0))


def _sgu_layer(x, prev, mod, g, w_in, ln_g, ln_b, w_s, b_st, w_out, router_p, n_exp, bufs, row0, has_v):
    b, l, d = x.shape
    tm = _pick(256, l)
    n_blk = l // tm
    blk0 = row0 // tm
    y_all, mod_prev, g_prev = prev
    row_map = lambda bi, i: (blk0 + bi * n_blk + i, 0)
    mod_spec = pl.BlockSpec((1, N_MOD, d), lambda bi, i: (bi, 0, 0))

    in_specs = [pl.BlockSpec((1, tm, d), lambda bi, i: (bi, i, 0)),
                pl.BlockSpec((tm, d), row_map), mod_spec, _resident(g_prev.shape),
                mod_spec, _resident(g.shape)]
    args = [x, y_all, mod_prev, g_prev, mod, g]
    for wt in (w_in, ln_g, ln_b, w_s, b_st, w_out, router_p):
        in_specs.append(_resident(wt.shape))
        args.append(wt)

    out_specs = [pl.BlockSpec((1, tm, d), lambda bi, i: (bi, i, 0)),
                 pl.BlockSpec((tm, d // 2), row_map),
                 pl.BlockSpec((tm, LANES), row_map),
                 pl.BlockSpec((tm, LANES), row_map)]
    out_shape = [jax.ShapeDtypeStruct((b, l, d), F32)]
    out_shape += [jax.ShapeDtypeStruct(buf.shape, buf.dtype) for buf in bufs]
    if has_v:
        out_specs.append(pl.BlockSpec((1, tm, d), lambda bi, i: (bi, i, 0)))
        out_shape.append(jax.ShapeDtypeStruct((b, l, d), F32))

    aliases = {}
    n_in = len(args)
    for k, buf in enumerate(bufs):
        in_specs.append(pl.BlockSpec(memory_space=pl.ANY))
        args.append(buf)
        aliases[n_in + k] = 1 + k
    n_alias = len(aliases)

    def kern(*refs):
        refs = refs[:n_in] + refs[n_in + n_alias:]
        _sgu_kernel(*refs, has_v=has_v, tm=tm, n_exp=n_exp)

    return pl.pallas_call(
        kern,
        grid=(b, n_blk),
        in_specs=in_specs,
        out_specs=out_specs,
        out_shape=out_shape,
        input_output_aliases=aliases,
        compiler_params=_params("arbitrary", "arbitrary"),
        name="sgu_v" if has_v else "sgu",
    )(*args)


def _route_kernel(idx_ref, pos_ref, meta_ref, cnt_ref, carry_ref, start_ref, *, tb, tm_g, n_exp, nbl):
    p, i = pl.program_id(0), pl.program_id(1)
    lane = lax.broadcasted_iota(jnp.int32, (tb, LANES), 1)
    idx = idx_ref[...]
    oh1 = lane == idx[:, 0:1]
    oh2 = lane == idx[:, 1:2]
    m = jnp.where(oh1 | oh2, 1.0, 0.0)
    colsum = jnp.sum(m, axis=0, keepdims=True)

    @pl.when((p == 0) & (i == 0))
    def _():
        cnt_ref[...] = jnp.zeros((1, LANES), F32)

    @pl.when(p == 0)
    def _():
        cnt_ref[...] += colsum

    @pl.when((p == 1) & (i == 0))
    def _():
        cnt = cnt_ref[...]
        padded = jnp.ceil(cnt * (1.0 / tm_g)) * float(tm_g)
        lane1 = lax.broadcasted_iota(jnp.int32, (1, LANES), 1)
        blk_row = lax.broadcasted_iota(jnp.int32, (1, nbl), 1).astype(F32) * float(tm_g)
        run = jnp.zeros((1, 1), F32)
        starts = jnp.zeros((1, LANES), F32)
        ends = jnp.zeros((1, LANES), F32)
        blk_exp = jnp.zeros((1, nbl), F32)
        blk_rows = jnp.zeros((1, nbl), F32)
        for ex in range(n_exp):
            starts = jnp.where(lane1 == ex, run, starts)
            real_end = run + jnp.sum(jnp.where(lane1 == ex, cnt, 0.0), axis=1, keepdims=True)
            in_group = blk_row >= run
            run = run + jnp.sum(jnp.where(lane1 == ex, padded, 0.0), axis=1, keepdims=True)
            in_group = in_group & (blk_row < run)
            ends = jnp.where(lane1 == ex, run, ends)
            blk_exp = blk_exp + jnp.where(blk_row >= run, 1.0, 0.0)
            blk_rows = jnp.where(in_group, jnp.minimum(real_end - blk_row, float(tm_g)), blk_rows)
        start_ref[...] = starts
        carry_ref[...] = jnp.zeros((1, LANES), F32)
        meta_ref[...] = jnp.zeros(meta_ref.shape, jnp.int32)
        meta_ref[0:1, :] = jnp.minimum(blk_exp, float(n_exp - 1)).astype(jnp.int32)
        meta_ref[1:2, :] = jnp.broadcast_to(run * (1.0 / tm_g), (1, nbl)).astype(jnp.int32)
        meta_ref[2:3, 0:LANES] = (starts + cnt).astype(jnp.int32)
        meta_ref[3:4, 0:LANES] = ends.astype(jnp.int32)
        meta_ref[4:5, :] = blk_rows.astype(jnp.int32)

    @pl.when(p == 1)
    def _():
        r = lax.broadcasted_iota(jnp.int32, (tb, tb), 0)
        c = lax.broadcasted_iota(jnp.int32, (tb, tb), 1)
        tri = jnp.where(c < r, 1.0, 0.0).astype(BF16)
        rank = _dot(tri, m.astype(BF16)) + carry_ref[...]
        dest = start_ref[...] + rank
        p1 = jnp.sum(jnp.where(oh1, dest, 0.0), axis=1, keepdims=True)
        p2 = jnp.sum(jnp.where(oh2, dest, 0.0), axis=1, keepdims=True)
        pos_ref[...] = jnp.where(lane == 0, p1, jnp.where(lane == 1, p2, 0.0)).astype(jnp.int32)
        carry_ref[...] += colsum


def _route(idx_all, tm_g, n_exp, nbl):
    t = idx_all.shape[0]
    tb = _pick(1024, t)
    return pl.pallas_call(
        functools.partial(_route_kernel, tb=tb, tm_g=tm_g, n_exp=n_exp, nbl=nbl),
        grid=(2, t // tb),
        in_specs=[pl.BlockSpec((tb, LANES), lambda p, i: (i, 0))],
        out_specs=[pl.BlockSpec((tb, LANES), lambda p, i: (p * i, 0)),
                   pl.BlockSpec((SUBLANES, nbl), lambda p, i: (0, 0))],
        out_shape=[jax.ShapeDtypeStruct((t, LANES), jnp.int32),
                   jax.ShapeDtypeStruct((SUBLANES, nbl), jnp.int32)],
        scratch_shapes=[pltpu.VMEM((1, LANES), F32)] * 3,
        compiler_params=_params("arbitrary", "arbitrary"),
        name="route",
    )(idx_all)


def _dispatch_kernel(lo_ref, hi_ref, nu_ref, pos0_ref, pos1_ref, h_ref, xs_ref, sem, *, tmd, n_exp, tm_g, n_blocks):
    i = pl.program_id(0)

    def row_copy(src_row, dst_row):
        return pltpu.make_async_copy(h_ref.at[pl.ds(src_row, 1)], xs_ref.at[pl.ds(dst_row, 1)], sem)

    def rows_wait(n):
        pltpu.make_async_copy(h_ref.at[pl.ds(0, n)], xs_ref.at[pl.ds(0, n)], sem).wait()

    @pl.when(i == 0)
    def _():
        def fill_block(b, c):
            cp = pltpu.make_async_copy(h_ref.at[pl.ds(0, tm_g)], xs_ref.at[pl.ds(b * tm_g, tm_g)], sem)
            cp.start()
            cp.wait()
            return c

        lax.fori_loop(nu_ref[0], n_blocks, fill_block, 0)

        for ex in range(n_exp):
            lo, hi = lo_ref[ex], hi_ref[ex]

            def fill(r, c):
                row_copy(0, r).start()
                return c

            def drain(r, c):
                row_copy(0, r).wait()
                return c

            lax.fori_loop(lo, hi, fill, 0)
            lax.fori_loop(lo, hi, drain, 0)

    def issue(r, c):
        row_copy(r, pos0_ref[r]).start(priority=0)
        row_copy(r, pos1_ref[r]).start(priority=1)
        return c

    lax.fori_loop(0, tmd, issue, 0)
    for _ in range(TOP_K):
        rows_wait(tmd)


def _dispatch(pad_lo, pad_hi, n_used, pos0, pos1, hp_all, tm_g, n_blocks, n_exp):
    t, dw = hp_all.shape
    n_rows = n_blocks * tm_g
    tmd = _pick(1024, t)
    assert tmd >= tm_g
    return pl.pallas_call(
        functools.partial(_dispatch_kernel, tmd=tmd, n_exp=n_exp, tm_g=tm_g, n_blocks=n_blocks),
        grid_spec=pltpu.PrefetchScalarGridSpec(
            num_scalar_prefetch=3,
            grid=(t // tmd,),
            in_specs=[pl.BlockSpec((tmd,), lambda i, *_: (i,), memory_space=pltpu.SMEM),
                      pl.BlockSpec((tmd,), lambda i, *_: (i,), memory_space=pltpu.SMEM),
                      pl.BlockSpec((tmd, dw), lambda i, *_: (i, 0))],
            out_specs=pl.BlockSpec(memory_space=pl.ANY),
            scratch_shapes=[pltpu.SemaphoreType.DMA(())],
        ),
        out_shape=jax.ShapeDtypeStruct((n_rows, dw), hp_all.dtype),
        compiler_params=pltpu.CompilerParams(dimension_semantics=("arbitrary",), has_side_effects=True),
        name="dispatch",
    )(pad_lo, pad_hi, n_used, pos0, pos1, hp_all)


def _moe_kernel(be_ref, nu_ref, nv_ref, xs_ref, wg_ref, wu_ref, wd_ref, o_ref, xb_ref):
    b, j = pl.program_id(0), pl.program_id(1)
    tm, dw = xs_ref.shape

    @pl.when(j == 0)
    def _():
        o_ref[...] = jnp.zeros(o_ref.shape, F32)

    def expert_rows(rows):
        @pl.when(j == 0)
        def _():
            w = xs_ref[0:rows, :]
            xb_ref[0:rows, :dw] = lax.bitcast_convert_type(w << 16, F32).astype(BF16)
            xb_ref[0:rows, dw:] = lax.bitcast_convert_type(w & jnp.uint32(0xFFFF0000), F32).astype(BF16)

        x = xb_ref[0:rows, :]
        gt = _dot(x, wg_ref[...].astype(BF16))
        up = _dot(x, wu_ref[...].astype(BF16))
        a = (gt * _sigmoid(gt) * up).astype(BF16)
        o_ref[0:rows, :] += _dot(a, wd_ref[...].astype(BF16))

    live = b < nu_ref[0]
    half = nv_ref[b] <= tm // 2

    @pl.when(live & jnp.logical_not(half))
    def _():
        expert_rows(tm)

    @pl.when(live & half)
    def _():
        expert_rows(tm // 2)


def _moe(blk_exp, n_used, blk_rows, xs, w_gate, w_up, w_down, jl, tm_g):
    n_rows, dw = xs.shape
    _, _, d, f = w_gate.shape
    tf = _pick(512, f)
    n_j = f // tf

    def live(b, nu):
        return jnp.minimum(b, nu[0] - 1)

    def jj(b, j, nu):
        return jnp.where(b < nu[0], j, n_j - 1)

    return pl.pallas_call(
        _moe_kernel,
        grid_spec=pltpu.PrefetchScalarGridSpec(
            num_scalar_prefetch=3,
            grid=(n_rows // tm_g, n_j),
            in_specs=[pl.BlockSpec((tm_g, dw), lambda b, j, be, nu, nv: (live(b, nu), 0)),
                      pl.BlockSpec((None, None, d, tf),
                                   lambda b, j, be, nu, nv: (jl, be[live(b, nu)], 0, jj(b, j, nu))),
                      pl.BlockSpec((None, None, d, tf),
                                   lambda b, j, be, nu, nv: (jl, be[live(b, nu)], 0, jj(b, j, nu))),
                      pl.BlockSpec((None, None, tf, d),
                                   lambda b, j, be, nu, nv: (jl, be[live(b, nu)], jj(b, j, nu), 0))],
            out_specs=pl.BlockSpec((tm_g, d), lambda b, j, be, nu, nv: (b, 0)),
            scratch_shapes=[pltpu.VMEM((tm_g, d), BF16)],
        ),
        out_shape=jax.ShapeDtypeStruct((n_rows, d), F32),
        compiler_params=_params("arbitrary", "arbitrary"),
        name="moe",
    )(blk_exp, n_used, blk_rows, xs, w_gate, w_up, w_down)


def _combine_kernel(pos0_ref, pos1_ref, gate_ref, ys_ref, o_ref, buf_ref, sem, *, tmc):
    def row_copy(k, src_row, r):
        return pltpu.make_async_copy(ys_ref.at[pl.ds(src_row, 1)], buf_ref.at[k, pl.ds(r, 1)], sem)

    def issue(r, c):
        row_copy(0, pos0_ref[r], r).start(priority=0)
        row_copy(1, pos1_ref[r], r).start(priority=1)
        return c

    lax.fori_loop(0, tmc, issue, 0)
    for k in range(TOP_K):
        pltpu.make_async_copy(ys_ref.at[pl.ds(0, tmc)], buf_ref.at[k], sem).wait()
    gate = gate_ref[...]
    o_ref[...] = gate[:, 0:1] * buf_ref[0] + gate[:, 1:2] * buf_ref[1]


def _combine(pos0, pos1, gate_all, ys):
    t = gate_all.shape[0]
    d = ys.shape[1]
    tmc = _pick(1024, t)
    return pl.pallas_call(
        functools.partial(_combine_kernel, tmc=tmc),
        grid=(t // tmc,),
        in_specs=[pl.BlockSpec((tmc,), lambda i: (i,), memory_space=pltpu.SMEM),
                  pl.BlockSpec((tmc,), lambda i: (i,), memory_space=pltpu.SMEM),
                  pl.BlockSpec((tmc, LANES), lambda i: (i, 0)),
                  pl.BlockSpec(memory_space=pl.ANY)],
        out_specs=pl.BlockSpec((tmc, d), lambda i: (i, 0)),
        out_shape=jax.ShapeDtypeStruct((t, d), F32),
        scratch_shapes=[pltpu.VMEM((TOP_K, tmc, d), F32), pltpu.SemaphoreType.DMA(())],
        compiler_params=_params("arbitrary"),
        name="combine",
    )(pos0, pos1, gate_all, ys)


def _final_kernel(x_ref, y_ref, mod_ref, g_ref, o_ref):
    o_ref[0] = x_ref[0] + _rms(y_ref[...], mod_ref[0, 5:6, :] * g_ref[3:4, :])


def _final(x, y_all, mod, g, row0):
    b, l, d = x.shape
    tm = _pick(512, l)
    n_blk = l // tm
    blk0 = row0 // tm
    return pl.pallas_call(
        _final_kernel,
        grid=(b, n_blk),
        in_specs=[pl.BlockSpec((1, tm, d), lambda bi, i: (bi, i, 0)),
                  pl.BlockSpec((tm, d), lambda bi, i: (blk0 + bi * n_blk + i, 0)),
                  pl.BlockSpec((1, N_MOD, d), lambda bi, i: (bi, 0, 0)),
                  _resident(g.shape)],
        out_specs=pl.BlockSpec((1, tm, d), lambda bi, i: (bi, i, 0)),
        out_shape=jax.ShapeDtypeStruct((b, l, d), F32),
        compiler_params=_params("arbitrary", "arbitrary"),
        name="final",
    )(x, y_all, mod, g)


def kernel(x_prompt, x_sample, c_prompt, c_sample, state_pool, ada_w, ada_b, norm_g, pool_w_in, pool_w_grp, pool_scale, pool_w_out, sgu_w_in, sgu_ln_g, sgu_ln_b, sgu_w_s, sgu_b_s, sgu_w_out, ffn_w_gate, ffn_w_up, ffn_w_down, moe_router, moe_w_gate, moe_w_up, moe_w_down):
    bp, lp, d = x_prompt.shape
    bs, ls, _ = x_sample.shape
    depth = ada_w.shape[0]
    n_exp = moe_router.shape[-1]
    tp, ts = bp * lp, bs * ls
    t_all = tp + ts
    assert 2 * n_exp <= LANES and state_pool.shape[2] == POOL_STATE and ls <= SGU_BLOCK

    tm_g = _pick(1024, TOP_K * t_all)
    n_blocks = -(-TOP_K * t_all // tm_g) + n_exp
    nbl = -(-n_blocks // LANES) * LANES

    mods = _ada(jnp.concatenate([c_prompt, c_sample], axis=0), ada_w, ada_b)
    mods = mods.reshape(depth, bp + bs, N_MOD, d)
    mod_p = [mods[l, :bp] for l in range(depth)]
    mod_s = [mods[l, bp:] for l in range(depth)]
    hist = jnp.pad(state_pool, ((0, 0), (0, 0), (HIST_ROWS - POOL_STATE, 0), (0, 0)))

    h_all = jnp.zeros((t_all, d), BF16)
    route_bufs = (jnp.zeros((t_all, d // 2), jnp.uint32), jnp.zeros((t_all, LANES), jnp.int32),
                  jnp.zeros((t_all, LANES), F32))

    ffn_w = (ffn_w_gate, ffn_w_up, ffn_w_down)
    moe_w = (moe_w_gate, moe_w_up, moe_w_down)

    xp, xs = x_prompt, x_sample
    y_all = None
    pool_p, pool_s, sgu_s = [], [], []
    for l in range(depth):
        jl = l // 2
        g = norm_g[l]
        prev_p = prev_s = None
        if l > 0:
            prev_p = (y_all, mod_p[l - 1], norm_g[l - 1])
            prev_s = (y_all, mod_s[l - 1], norm_g[l - 1])
        if l % 2 == 0:
            w = (pool_w_in[jl].astype(BF16), pool_w_grp[jl].astype(BF16),
                 pool_scale[jl].reshape(1, d), pool_w_out[jl].astype(BF16))
            xp, h_all, st_p = _pool_layer(xp, prev_p, mod_p[l], g, None, *w, h_all, 0)
            xs, h_all, st_s = _pool_layer(xs, prev_s, mod_s[l], g, hist[jl], *w, h_all, tp)
            pool_p.append(st_p[:, HIST_ROWS - POOL_STATE:])
            pool_s.append(st_s[:, HIST_ROWS - POOL_STATE:])
            y_all = _ffn(h_all, *ffn_w, jl)
        else:
            r_hi = moe_router[jl].astype(BF16)
            r_lo = (moe_router[jl] - r_hi.astype(F32)).astype(BF16)
            router_p = jnp.pad(jnp.concatenate([r_hi, r_lo], axis=1), ((0, 0), (0, LANES - 2 * n_exp)))
            w = (sgu_w_in[jl].astype(BF16), sgu_ln_g[jl].reshape(1, d), sgu_ln_b[jl].reshape(1, d),
                 sgu_w_s[jl], jnp.transpose(sgu_b_s[jl]), sgu_w_out[jl].astype(BF16), router_p, n_exp)
            xp, *route_bufs = _sgu_layer(xp, prev_p, mod_p[l], g, *w, route_bufs, 0, False)
            xs, *route_bufs, v_s = _sgu_layer(xs, prev_s, mod_s[l], g, *w, route_bufs, tp, True)
            hp_all, idx_all, gate_all = route_bufs
            sgu_s.append(v_s)
            pos_all, meta = _route(idx_all, tm_g, n_exp, nbl)
            pos0, pos1 = pos_all[:, 0], pos_all[:, 1]
            n_used = meta[1, :1]
            xsort = _dispatch(meta[2, :n_exp], meta[3, :n_exp], n_used, pos0, pos1, hp_all, tm_g, n_blocks, n_exp)
            ys = _moe(meta[0, :n_blocks], n_used, meta[4, :n_blocks], xsort, *moe_w, jl, tm_g)
            y_all = _combine(pos0, pos1, gate_all, ys)
    xp = _final(xp, y_all, mod_p[depth - 1], norm_g[depth - 1], 0)
    xs = _final(xs, y_all, mod_s[depth - 1], norm_g[depth - 1], tp)
    return (xp, xs, jnp.stack(pool_p, axis=0), jnp.stack(pool_s, axis=0), jnp.stack(sgu_s, axis=0))
```
